```python
import math
import jax, jax.numpy as jnp
from jax import lax
import numpy as np

D_MODEL = 1024
BATCH = 4
SEQ = 8192
DEPTH = 2

EPS = 1e-6

SSD_EXPAND = 2
D_INNER = SSD_EXPAND * D_MODEL
SSD_HEAD_DIM = 64
SSD_HEADS = D_INNER // SSD_HEAD_DIM
SSD_GROUPS = 8
SSD_HPG = SSD_HEADS // SSD_GROUPS
D_STATE = 128
CONV_K = 4
CONV_DIM = D_INNER + 2 * SSD_GROUPS * D_STATE
CHUNK = 128

ATT_PATTERNS = ((128, 1), (512, 4), (2048, 16))
ATT_GROUPS = 3
ATT_HEADS = 16
ATT_HEAD_DIM = 64
ATT_W = ATT_HEADS * ATT_HEAD_DIM
ATT_BLOCK = 128

MEM_LEN = 256
MEM_HEADS = 4
MEM_HEAD_DIM = D_MODEL // MEM_HEADS
MEM_W = MEM_HEADS * MEM_HEAD_DIM

N_BRANCH = 3

OFF_ZSSD = 0
OFF_XBC = OFF_ZSSD + D_INNER
OFF_DT = OFF_XBC + CONV_DIM
OFF_QKV = OFF_DT + SSD_HEADS
OFF_ZATT = OFF_QKV + 3 * ATT_GROUPS * ATT_W
OFF_QMEM = OFF_ZATT + ATT_W
OFF_ZMEM = OFF_QMEM + MEM_W
OFF_GATE = OFF_ZMEM + MEM_W
N_IN = OFF_GATE + N_BRANCH * D_MODEL

kernel_name = "hybrid_ssd_dilated_memory_block"


def _rmsnorm(x, g):
    xf = x.astype(jnp.float32)
    y = xf * lax.rsqrt(jnp.mean(xf * xf, axis=-1, keepdims=True) + EPS)
    return (y * g.astype(jnp.float32)).astype(x.dtype)


def _proj(h, w, start, size):
    return h @ w[:, start:start + size]


def _causal_depthwise_conv(u, w, b):
    c = u.shape[-1]
    y = lax.conv_general_dilated(u, w[:, None, :].astype(u.dtype), window_strides=(1,),
                                 padding=((CONV_K - 1, 0),),
                                 dimension_numbers=('NWC', 'WIO', 'NWC'),
                                 feature_group_count=c)
    return y + b.astype(u.dtype)


def _ssd_chunked(xs, dt, a, bm, cm):
    b, s, nh, p = xs.shape
    nc = s // CHUNK
    x = (xs * dt[..., None]).reshape(b, nc, CHUNK, SSD_GROUPS, SSD_HPG, p)
    la = (dt * a).reshape(b, nc, CHUNK, SSD_GROUPS, SSD_HPG)
    bc = bm.reshape(b, nc, CHUNK, SSD_GROUPS, D_STATE)
    cc = cm.reshape(b, nc, CHUNK, SSD_GROUPS, D_STATE)
    a_cum = jnp.cumsum(la, axis=2)
    ac = jnp.moveaxis(a_cum, 2, -1)
    causal = jnp.tril(jnp.ones((CHUNK, CHUNK), dtype=bool))
    seg = ac[..., :, None] - ac[..., None, :]
    decay = jnp.exp(jnp.where(causal, seg, -jnp.inf))
    cb = jnp.einsum('bclgn,bcsgn->bcgls', cc, bc)
    y_diag = jnp.einsum('bcgels,bcsgep->bclgep', cb[:, :, :, None] * decay, x)
    decay_states = jnp.exp(a_cum[:, :, -1:] - a_cum)
    states = jnp.einsum('bcsgn,bcsgep->bcgepn', bc, x * decay_states[..., None])
    chunk_decay = jnp.exp(a_cum[:, :, -1])

    def step(hst, inp):
        st, dec = inp
        return hst * dec[..., None, None] + st, hst

    h0 = jnp.zeros((b, SSD_GROUPS, SSD_HPG, p, D_STATE), jnp.float32)
    _, prev = lax.scan(step, h0, (jnp.swapaxes(states, 0, 1), jnp.swapaxes(chunk_decay, 0, 1)))
    prev = jnp.swapaxes(prev, 0, 1)
    y_off = jnp.einsum('bclgn,bcgepn->bclgep', cc, prev) * jnp.exp(a_cum)[..., None]
    return (y_diag + y_off).reshape(b, s, nh, p)


def _ssd_branch(h, w, conv_w, conv_b, dt_bias, a_log, d_skip, ssd_norm, w_ssd_out):
    bsz, s, _ = h.shape
    z = _proj(h, w, OFF_ZSSD, D_INNER)
    xbc = jax.nn.silu(_causal_depthwise_conv(_proj(h, w, OFF_XBC, CONV_DIM), conv_w, conv_b))
    dt_raw = _proj(h, w, OFF_DT, SSD_HEADS)
    xs = xbc[..., :D_INNER].reshape(bsz, s, SSD_HEADS, SSD_HEAD_DIM).astype(jnp.float32)
    bm = xbc[..., D_INNER:D_INNER + SSD_GROUPS * D_STATE].reshape(bsz, s, SSD_GROUPS, D_STATE).astype(jnp.float32)
    cm = xbc[..., D_INNER + SSD_GROUPS * D_STATE:].reshape(bsz, s, SSD_GROUPS, D_STATE).astype(jnp.float32)
    dt = jax.nn.softplus(dt_raw.astype(jnp.float32) + dt_bias.astype(jnp.float32))
    a = -jnp.exp(a_log.astype(jnp.float32))
    y = _ssd_chunked(xs, dt, a, bm, cm) + d_skip.astype(jnp.float32)[:, None] * xs
    y = y.reshape(bsz, s, D_INNER) * jax.nn.silu(z.astype(jnp.float32))
    yg = y.reshape(bsz, s, SSD_GROUPS, D_INNER // SSD_GROUPS)
    yg = yg * lax.rsqrt(jnp.mean(yg * yg, axis=-1, keepdims=True) + EPS)
    y = yg.reshape(bsz, s, D_INNER) * ssd_norm.astype(jnp.float32)
    return y.astype(h.dtype) @ w_ssd_out


def _dilated_window_attention(q, k, v, window, dilation):
    b, s, hh, e = q.shape
    n_off = window // dilation
    L = s // dilation
    Lp = -(-L // ATT_BLOCK) * ATT_BLOCK
    nb = Lp // ATT_BLOCK

    def to_streams(t):
        t = t.reshape(b, L, dilation, hh, e).transpose(0, 2, 1, 3, 4)
        t = jnp.pad(t, ((0, 0), (0, 0), (0, Lp - L), (0, 0), (0, 0)))
        return t.reshape(b, dilation, nb, ATT_BLOCK, hh, e)

    def with_prev(t):
        prev = jnp.pad(t, ((0, 0), (0, 0), (1, 0), (0, 0), (0, 0), (0, 0)))[:, :, :-1]
        return jnp.concatenate([prev, t], axis=3)

    qb = to_streams(q)
    kk = with_prev(to_streams(k))
    vv = with_prev(to_streams(v))
    scores = jnp.einsum('brnqhe,brnkhe->brnhqk', qb, kk).astype(jnp.float32) * (e ** -0.5)
    qi = jnp.arange(ATT_BLOCK)[:, None]
    kj = jnp.arange(2 * ATT_BLOCK)[None, :]
    dist = qi - kj + ATT_BLOCK
    kglob = jnp.arange(nb)[:, None, None] * ATT_BLOCK + kj[None] - ATT_BLOCK
    valid = (dist >= 0)[None] & (dist <= n_off)[None] & (kglob >= 0)
    scores = jnp.where(valid[None, None, :, None], scores, -jnp.inf)
    mx = jnp.max(scores, axis=-1, keepdims=True)
    pr = jnp.exp(scores - mx)
    den = jnp.sum(pr, axis=-1, keepdims=True)
    o = jnp.einsum('brnhqk,brnkhe->brnqhe', pr, vv.astype(jnp.float32)) / jnp.swapaxes(den, 3, 4)
    lse = jnp.swapaxes((mx + jnp.log(den))[..., 0], 3, 4)
    o = o.reshape(b, dilation, Lp, hh, e)[:, :, :L].transpose(0, 2, 1, 3, 4).reshape(b, s, hh, e)
    lse = lse.reshape(b, dilation, Lp, hh)[:, :, :L].transpose(0, 2, 1, 3).reshape(b, s, hh)
    return o, lse


def _dilated_branch(h, w, w_attn_out):
    bsz, s, _ = h.shape
    qkv = _proj(h, w, OFF_QKV, 3 * ATT_GROUPS * ATT_W).reshape(
        bsz, s, 3, ATT_GROUPS, ATT_HEADS, ATT_HEAD_DIM)
    outs, lses = [], []
    for g, (window, dilation) in enumerate(ATT_PATTERNS):
        o, lse = _dilated_window_attention(qkv[:, :, 0, g], qkv[:, :, 1, g], qkv[:, :, 2, g],
                                           window, dilation)
        outs.append(o)
        lses.append(lse)
    wts = jax.nn.softmax(jnp.stack(lses, axis=0), axis=0)
    o = jnp.sum(wts[..., None] * jnp.stack(outs, axis=0), axis=0)
    z = _proj(h, w, OFF_ZATT, ATT_W)
    y = o.reshape(bsz, s, ATT_W) * jax.nn.silu(z.astype(jnp.float32))
    return y.astype(h.dtype) @ w_attn_out


def _memory_branch(h, w, mem, mem_norm, w_mem_kv, w_mem_out):
    bsz, s, _ = h.shape
    q = _proj(h, w, OFF_QMEM, MEM_W).reshape(bsz, s, MEM_HEADS, MEM_HEAD_DIM)
    kv = (_rmsnorm(mem, mem_norm) @ w_mem_kv).reshape(bsz, MEM_LEN, 2, MEM_HEADS, MEM_HEAD_DIM)
    scores = jnp.einsum('bshe,bmhe->bhsm', q, kv[:, :, 0]).astype(jnp.float32) * (MEM_HEAD_DIM ** -0.5)
    pr = jax.nn.softmax(scores, axis=-1)
    o = jnp.einsum('bhsm,bmhe->bshe', pr, kv[:, :, 1].astype(jnp.float32)).reshape(bsz, s, MEM_W)
    z = _proj(h, w, OFF_ZMEM, MEM_W)
    y = o * jax.nn.silu(z.astype(jnp.float32))
    return y.astype(h.dtype) @ w_mem_out


def setup_inputs(seed: int = 0) -> dict:
    key = jax.random.key(seed)
    ks = jax.random.split(key, 20)
    f32 = jnp.float32
    x = jax.random.normal(ks[0], (BATCH, SEQ, D_MODEL), f32)
    mem = jax.random.normal(ks[1], (BATCH, MEM_LEN, D_MODEL), f32)
    norm_pre = 1.0 + 0.02 * jax.random.normal(ks[2], (DEPTH, D_MODEL), f32)
    norm_post = 1.0 + 0.02 * jax.random.normal(ks[3], (DEPTH, D_MODEL), f32)
    w_in = jax.random.normal(ks[4], (DEPTH, D_MODEL, N_IN), f32) * D_MODEL ** -0.5
    conv_w = jax.random.normal(ks[5], (DEPTH, CONV_K, CONV_DIM), f32) * CONV_K ** -0.5
    conv_b = 0.02 * jax.random.normal(ks[6], (DEPTH, CONV_DIM), f32)
    dt0 = jnp.exp(jax.random.uniform(ks[7], (DEPTH, SSD_HEADS), f32,
                                     math.log(1e-3), math.log(1e-1)))
    dt_bias = dt0 + jnp.log(-jnp.expm1(-dt0))
    a_log = jnp.log(jax.random.uniform(ks[8], (DEPTH, SSD_HEADS), f32, 1.0, 16.0))
    d_skip = 1.0 + 0.02 * jax.random.normal(ks[9], (DEPTH, SSD_HEADS), f32)
    ssd_norm = 1.0 + 0.02 * jax.random.normal(ks[10], (DEPTH, D_INNER), f32)
    w_ssd_out = jax.random.normal(ks[11], (DEPTH, D_INNER, D_MODEL), f32) * D_INNER ** -0.5
    w_attn_out = jax.random.normal(ks[12], (DEPTH, ATT_W, D_MODEL), f32) * ATT_W ** -0.5
    mem_norm = 1.0 + 0.02 * jax.random.normal(ks[13], (DEPTH, D_MODEL), f32)
    w_mem_kv = jax.random.normal(ks[14], (DEPTH, D_MODEL, 2 * MEM_W), f32) * D_MODEL ** -0.5
    w_mem_out = jax.random.normal(ks[15], (DEPTH, MEM_W, D_MODEL), f32) * MEM_W ** -0.5
    w_out = jax.random.normal(ks[16], (DEPTH, D_MODEL, D_MODEL), f32) * D_MODEL ** -0.5
    return {"x": x, "mem": mem, "norm_pre": norm_pre, "norm_post": norm_post, "w_in": w_in,
            "conv_w": conv_w, "conv_b": conv_b, "dt_bias": dt_bias, "a_log": a_log,
            "d_skip": d_skip, "ssd_norm": ssd_norm, "w_ssd_out": w_ssd_out,
            "w_attn_out": w_attn_out, "mem_norm": mem_norm, "w_mem_kv": w_mem_kv,
            "w_mem_out": w_mem_out, "w_out": w_out}


def reference(x, mem, norm_pre, norm_post, w_in, conv_w, conv_b, dt_bias, a_log, d_skip,
              ssd_norm, w_ssd_out, w_attn_out, mem_norm, w_mem_kv, w_mem_out, w_out):
    bsz, s, _ = x.shape
    for l in range(DEPTH):
        h = _rmsnorm(x, norm_pre[l])
        w = w_in[l]
        y_ssd = _ssd_branch(h, w, conv_w[l], conv_b[l], dt_bias[l], a_log[l], d_skip[l],
                            ssd_norm[l], w_ssd_out[l])
        y_att = _dilated_branch(h, w, w_attn_out[l])
        y_mem = _memory_branch(h, w, mem, mem_norm[l], w_mem_kv[l], w_mem_out[l])
        gates = jax.nn.sigmoid(_proj(h, w, OFF_GATE, N_BRANCH * D_MODEL)).reshape(
            bsz, s, N_BRANCH, D_MODEL)
        merged = gates[:, :, 0] * y_ssd + gates[:, :, 1] * y_att + gates[:, :, 2] * y_mem
        out = merged @ w_out[l]
        x = x + _rmsnorm(out, norm_post[l])
    return x
```

```python
import functools

import jax
import jax.numpy as jnp
from jax import lax
from jax.experimental import pallas as pl
from jax.experimental.pallas import tpu as pltpu

F32 = jnp.float32
BF16 = jnp.bfloat16

EPS = 1e-6
D_MODEL = 1024

D_INNER = 2048
SSD_HEAD_DIM = 64
SSD_HEADS = 32
SSD_GROUPS = 8
D_STATE = 128
CONV_K = 4
CONV_DIM = D_INNER + 2 * SSD_GROUPS * D_STATE
CHUNK = 128

ATT_PATTERNS = ((128, 1), (512, 4), (2048, 16))
ATT_GROUPS = 3
ATT_HEADS = 16
ATT_HEAD_DIM = 64
ATT_W = 1024
ATT_BLOCK = 128

MEM_LEN = 256
MEM_HEADS = 4
MEM_HEAD_DIM = 256
MEM_W = 1024

OFF_XBC = D_INNER
OFF_DT = OFF_XBC + CONV_DIM
OFF_QKV = OFF_DT + SSD_HEADS

N_MAIN = 21504
CB = 1024
BLK_Z = 0
BLK_X = 2
BLK_BC = 4
BLK_QKV = 6
BLK_ZATT = 15
BLK_QMEM = 16
BLK_ZMEM = 17
BLK_GATE = 18
N_BLK = N_MAIN // CB

LANES = 128
DT_PAD = LANES

VMEM_LIMIT = 56 * 1024 * 1024

NEG_INF = float("-inf")


def _sigmoid(v):
    return 1.0 / (1.0 + jnp.exp(-v))


def _silu(v):
    return v * _sigmoid(v)


def _split_dot(lhs_bf16, v):
    hi = v.astype(BF16)
    r1 = v - hi.astype(F32)
    mid = r1.astype(BF16)
    lo = (r1 - mid.astype(F32)).astype(BF16)
    acc = jnp.dot(lhs_bf16, hi, preferred_element_type=F32)
    acc = acc + jnp.dot(lhs_bf16, mid, preferred_element_type=F32)
    return acc + jnp.dot(lhs_bf16, lo, preferred_element_type=F32)


def _inproj_kernel(x_ref, g_ref, w_ref, wdt_ref, out_ref, dt_ref, h_ref):
    @pl.when(pl.program_id(1) == 0)
    def _():
        x = x_ref[...]
        ms = jnp.mean(x * x, axis=-1, keepdims=True)
        h = (x * lax.rsqrt(ms + EPS) * g_ref[...]).astype(BF16)
        h_ref[...] = h
        dt_ref[...] = jnp.dot(h, wdt_ref[...], preferred_element_type=F32)

    out_ref[...] = jnp.dot(h_ref[...], w_ref[...], preferred_element_type=F32).astype(BF16)


def _inproj(x2, gain, w_main, w_dt, tm):
    t = x2.shape[0]
    return pl.pallas_call(
        _inproj_kernel,
        grid=(t // tm, N_BLK),
        in_specs=[
            pl.BlockSpec((tm, D_MODEL), lambda i, j: (i, 0)),
            pl.BlockSpec((1, D_MODEL), lambda i, j: (0, 0)),
            pl.BlockSpec((D_MODEL, CB), lambda i, j: (0, j)),
            pl.BlockSpec((D_MODEL, DT_PAD), lambda i, j: (0, 0)),
        ],
        out_specs=[
            pl.BlockSpec((tm, CB), lambda i, j: (i, j)),
            pl.BlockSpec((tm, DT_PAD), lambda i, j: (i, 0)),
        ],
        out_shape=[
            jax.ShapeDtypeStruct((t, N_MAIN), BF16),
            jax.ShapeDtypeStruct((t, DT_PAD), F32),
        ],
        scratch_shapes=[pltpu.VMEM((tm, D_MODEL), BF16)],
        compiler_params=pltpu.CompilerParams(
            dimension_semantics=("parallel", "arbitrary"), vmem_limit_bytes=VMEM_LIMIT),
        name="inproj",
    )(x2, gain, w_main, w_dt)


def _memkv_kernel(m_ref, g_ref, w_ref, out_ref):
    m = m_ref[...]
    ms = jnp.mean(m * m, axis=-1, keepdims=True)
    h = (m * lax.rsqrt(ms + EPS) * g_ref[...]).astype(BF16)
    out_ref[...] = jnp.dot(h, w_ref[...], preferred_element_type=F32).astype(BF16)


def _memkv(mem2, gain, w_kv):
    rows = mem2.shape[0]
    return pl.pallas_call(
        _memkv_kernel,
        grid=(2 * MEM_W // CB,),
        in_specs=[
            pl.BlockSpec((rows, D_MODEL), lambda j: (0, 0)),
            pl.BlockSpec((1, D_MODEL), lambda j: (0, 0)),
            pl.BlockSpec((D_MODEL, CB), lambda j: (0, j)),
        ],
        out_specs=pl.BlockSpec((rows, CB), lambda j: (0, j)),
        out_shape=jax.ShapeDtypeStruct((rows, 2 * MEM_W), BF16),
        compiler_params=pltpu.CompilerParams(
            dimension_semantics=("arbitrary",), vmem_limit_bytes=VMEM_LIMIT),
        name="memkv",
    )(mem2, gain, w_kv)


N_PAIR = SSD_HEADS // 2
TAIL = 8


def _ssd_kernel(z_ref, x_ref, bc_ref, dt_ref, cw_ref, cb_ref, dtb_ref, alog_ref, dskip_ref,
                nrm_ref, y_ref, state_ref, cbuf_ref, act_ref):
    @pl.when(pl.program_id(1) == 0)
    def _():
        state_ref[...] = jnp.zeros_like(state_ref)
        cbuf_ref[0:TAIL, :] = jnp.zeros((TAIL, CONV_DIM), F32)

    cbuf_ref[TAIL:TAIL + CHUNK, 0:D_INNER] = x_ref[0].astype(F32)
    cbuf_ref[TAIL:TAIL + CHUNK, D_INNER:CONV_DIM] = bc_ref[0].astype(F32)
    conv = cb_ref[...]
    for k in range(CONV_K):
        lo = TAIL - (CONV_K - 1) + k
        conv = conv + cw_ref[k:k + 1, :] * cbuf_ref[lo:lo + CHUNK, :]
    cbuf_ref[0:TAIL, :] = cbuf_ref[CHUNK:CHUNK + TAIL, :]
    act_ref[...] = _silu(conv)

    v = dt_ref[0] + dtb_ref[...]
    dt = jnp.maximum(v, 0.0) + jnp.log1p(jnp.exp(-jnp.abs(v)))
    a = -jnp.exp(alog_ref[...])
    row = lax.broadcasted_iota(jnp.int32, (CHUNK, CHUNK), 0)
    col = lax.broadcasted_iota(jnp.int32, (CHUNK, CHUNK), 1)
    causal = row >= col
    tri = jnp.where(causal, 1.0, 0.0).astype(BF16)
    acum = _split_dot(tri, dt * a)
    total = acum[CHUNK - 1:CHUNK, :]
    acum_t = acum.T
    dt_t = dt.T
    dsts_t = (jnp.exp(total - acum) * dt).T
    cdec = jnp.exp(total)

    lane = lax.broadcasted_iota(jnp.int32, (CHUNK, LANES), 1)
    first = lane < SSD_HEAD_DIM

    for g in range(SSD_GROUPS):
        b_f = act_ref[:, D_INNER + g * D_STATE:D_INNER + (g + 1) * D_STATE]
        c_f = act_ref[:, D_INNER + (SSD_GROUPS + g) * D_STATE:D_INNER + (SSD_GROUPS + g + 1) * D_STATE]
        b_t = b_f.T
        cbm = lax.dot_general(c_f.astype(BF16), b_f.astype(BF16), (((1,), (1,)), ((), ())),
                              preferred_element_type=F32)
        y_pairs = []
        for pp in range(2):
            pair = 2 * g + pp
            x_pair = act_ref[:, pair * LANES:(pair + 1) * LANES]
            s_prev = state_ref[pair]
            rhs = jnp.concatenate([x_pair, s_prev], axis=0).astype(BF16)
            x_b = x_pair.astype(BF16)
            ys, sts, decs = [], [], []
            for e in range(2):
                h = 2 * pair + e
                acb = jnp.broadcast_to(acum[:, h:h + 1], (CHUNK, CHUNK))
                seg = acb - acum_t[h:h + 1, :]
                decay = jnp.exp(jnp.where(causal, seg, NEG_INF))
                m = cbm * decay * dt_t[h:h + 1, :]
                c_exp = c_f * jnp.exp(acb)
                lhs = jnp.concatenate([m, c_exp], axis=1).astype(BF16)
                ys.append(jnp.dot(lhs, rhs, preferred_element_type=F32))
                bts = (b_t * dsts_t[h:h + 1, :]).astype(BF16)
                sts.append(jnp.dot(bts, x_b, preferred_element_type=F32))
                decs.append(jnp.broadcast_to(cdec[:, h:h + 1], (CHUNK, LANES)))
            state_ref[pair] = (s_prev * jnp.where(first, decs[0], decs[1])
                               + jnp.where(first, sts[0], sts[1]))
            y_pair = jnp.where(first, ys[0], ys[1])
            y_pairs.append(y_pair + dskip_ref[:, pair * LANES:(pair + 1) * LANES] * x_pair)

        gsl = slice(2 * g * LANES, (2 * g + 2) * LANES)
        yg = jnp.concatenate(y_pairs, axis=1) * _silu(z_ref[0, :, gsl].astype(F32))
        ms = jnp.mean(yg * yg, axis=-1, keepdims=True)
        y_ref[0, :, gsl] = (yg * lax.rsqrt(ms + EPS) * nrm_ref[:, gsl]).astype(BF16)


def _ssd(proj3, dt3, conv_w, conv_b, dt_bias, a_log, d_skip_x, ssd_norm):
    b, s, _ = proj3.shape
    nc = s // CHUNK
    wide = D_INNER
    return pl.pallas_call(
        _ssd_kernel,
        grid=(b, nc),
        in_specs=[
            pl.BlockSpec((1, CHUNK, wide), lambda i, c: (i, c, BLK_Z * CB // wide)),
            pl.BlockSpec((1, CHUNK, wide), lambda i, c: (i, c, BLK_X * CB // wide)),
            pl.BlockSpec((1, CHUNK, wide), lambda i, c: (i, c, BLK_BC * CB // wide)),
            pl.BlockSpec((1, CHUNK, DT_PAD), lambda i, c: (i, c, 0)),
            pl.BlockSpec((CONV_K, CONV_DIM), lambda i, c: (0, 0)),
            pl.BlockSpec((1, CONV_DIM), lambda i, c: (0, 0)),
            pl.BlockSpec((1, DT_PAD), lambda i, c: (0, 0)),
            pl.BlockSpec((1, DT_PAD), lambda i, c: (0, 0)),
            pl.BlockSpec((1, D_INNER), lambda i, c: (0, 0)),
            pl.BlockSpec((1, D_INNER), lambda i, c: (0, 0)),
        ],
        out_specs=pl.BlockSpec((1, CHUNK, D_INNER), lambda i, c: (i, c, 0)),
        out_shape=jax.ShapeDtypeStruct((b, s, D_INNER), BF16),
        scratch_shapes=[
            pltpu.VMEM((N_PAIR, D_STATE, LANES), F32),
            pltpu.VMEM((TAIL + CHUNK, CONV_DIM), F32),
            pltpu.VMEM((CHUNK, CONV_DIM), F32),
        ],
        compiler_params=pltpu.CompilerParams(
            dimension_semantics=("parallel", "arbitrary"), vmem_limit_bytes=VMEM_LIMIT),
        name="ssd",
    )(proj3, proj3, proj3, dt3, conv_w, conv_b, dt_bias, a_log, d_skip_x, ssd_norm)


def _attn_kernel(q_ref, k_ref, v_ref, o_ref, lse_ref, kp_ref, vp_ref):
    n = pl.program_id(2)

    @pl.when(n == 0)
    def _():
        kp_ref[...] = jnp.zeros_like(kp_ref)
        vp_ref[...] = jnp.zeros_like(vp_ref)

    qi = lax.broadcasted_iota(jnp.int32, (ATT_BLOCK, 2 * ATT_BLOCK), 0)
    kj = lax.broadcasted_iota(jnp.int32, (ATT_BLOCK, 2 * ATT_BLOCK), 1)
    band = jnp.logical_and(kj >= qi, kj - ATT_BLOCK <= qi)
    first_key = jnp.where(n > 0, 0, ATT_BLOCK)
    valid = jnp.logical_and(band, kj >= first_key)
    lane = lax.broadcasted_iota(jnp.int32, (ATT_BLOCK, LANES), 1)
    first = lane < ATT_HEAD_DIM

    for hp in range(ATT_HEADS // 2):
        sl = slice(hp * LANES, (hp + 1) * LANES)
        q2 = q_ref[0, :, sl] * (ATT_HEAD_DIM ** -0.5)
        kk = jnp.concatenate([kp_ref[:, sl], k_ref[0, :, sl]], axis=0)
        vv = jnp.concatenate([vp_ref[:, sl], v_ref[0, :, sl]], axis=0)
        outs, lses = [], []
        for e in range(2):
            qh = jnp.where(first if e == 0 else jnp.logical_not(first), q2, jnp.zeros_like(q2))
            sc = lax.dot_general(qh, kk, (((1,), (1,)), ((), ())), preferred_element_type=F32)
            sc = jnp.where(valid, sc, NEG_INF)
            mx = jnp.max(sc, axis=-1, keepdims=True)
            p = jnp.exp(sc - mx)
            den = jnp.sum(p, axis=-1, keepdims=True)
            o = jnp.dot(p.astype(BF16), vv, preferred_element_type=F32)
            outs.append(o / den)
            lses.append(jnp.broadcast_to(mx + jnp.log(den), (ATT_BLOCK, LANES)))
        o_ref[0, :, sl] = jnp.where(first, outs[0], outs[1]).astype(BF16)
        lse_ref[0, :, sl] = jnp.where(first, lses[0], lses[1])

    kp_ref[...] = k_ref[0]
    vp_ref[...] = v_ref[0]


def _attention(proj3, g, dilation):
    b, s, _ = proj3.shape
    stream_len = s // dilation
    nb = stream_len // ATT_BLOCK
    view = proj3.reshape(b, stream_len, dilation * N_MAIN)

    def in_map(kind):
        return lambda i, r, n: (i, n, r * N_BLK + BLK_QKV + kind * ATT_GROUPS + g)

    out_map = lambda i, r, n: (i, n, r)
    o, lse = pl.pallas_call(
        _attn_kernel,
        grid=(b, dilation, nb),
        in_specs=[pl.BlockSpec((1, ATT_BLOCK, ATT_W), in_map(kind)) for kind in range(3)],
        out_specs=[
            pl.BlockSpec((1, ATT_BLOCK, ATT_W), out_map),
            pl.BlockSpec((1, ATT_BLOCK, ATT_W), out_map),
        ],
        out_shape=[
            jax.ShapeDtypeStruct((b, stream_len, dilation * ATT_W), BF16),
            jax.ShapeDtypeStruct((b, stream_len, dilation * ATT_W), F32),
        ],
        scratch_shapes=[
            pltpu.VMEM((ATT_BLOCK, ATT_W), BF16),
            pltpu.VMEM((ATT_BLOCK, ATT_W), BF16),
        ],
        compiler_params=pltpu.CompilerParams(
            dimension_semantics=("parallel", "parallel", "arbitrary"), vmem_limit_bytes=VMEM_LIMIT),
        name=f"attn_d{dilation}",
    )(view, view, view)
    return o.reshape(b * s, ATT_W), lse.reshape(b * s, ATT_W)


def _merge_kernel(x_ref, yssd_ref, o1_ref, o2_ref, o3_ref, l1_ref, l2_ref, l3_ref,
                  zatt_ref, qmem_ref, zmem_ref, g0_ref, g1_ref, g2_ref, kv_ref,
                  wso_ref, wao_ref, wmo_ref, wout_ref, npost_ref, out_ref):
    ssd = jnp.dot(yssd_ref[...], wso_ref[...], preferred_element_type=F32)
    merged = _sigmoid(g0_ref[...].astype(F32)) * ssd

    la, lb, lc = l1_ref[...], l2_ref[...], l3_ref[...]
    top = jnp.maximum(la, jnp.maximum(lb, lc))
    ea, eb, ec = jnp.exp(la - top), jnp.exp(lb - top), jnp.exp(lc - top)
    o = (ea * o1_ref[...].astype(F32) + eb * o2_ref[...].astype(F32)
         + ec * o3_ref[...].astype(F32)) / (ea + eb + ec)
    y_att = (o * _silu(zatt_ref[...].astype(F32))).astype(BF16)
    att = jnp.dot(y_att, wao_ref[...], preferred_element_type=F32)
    merged = merged + _sigmoid(g1_ref[...].astype(F32)) * att

    heads = []
    for h in range(MEM_HEADS):
        sl = slice(h * MEM_HEAD_DIM, (h + 1) * MEM_HEAD_DIM)
        q = qmem_ref[:, sl] * (MEM_HEAD_DIM ** -0.5)
        k = kv_ref[0, :, sl]
        vv = kv_ref[0, :, MEM_W + h * MEM_HEAD_DIM:MEM_W + (h + 1) * MEM_HEAD_DIM]
        sc = lax.dot_general(q, k, (((1,), (1,)), ((), ())), preferred_element_type=F32)
        mx = jnp.max(sc, axis=-1, keepdims=True)
        p = jnp.exp(sc - mx)
        den = jnp.sum(p, axis=-1, keepdims=True)
        heads.append(jnp.dot(p.astype(BF16), vv, preferred_element_type=F32) / den)
    o_mem = jnp.concatenate(heads, axis=1)
    y_mem = (o_mem * _silu(zmem_ref[...].astype(F32))).astype(BF16)
    mem = jnp.dot(y_mem, wmo_ref[...], preferred_element_type=F32)
    merged = merged + _sigmoid(g2_ref[...].astype(F32)) * mem

    out = jnp.dot(merged.astype(BF16), wout_ref[...], preferred_element_type=F32)
    ms = jnp.mean(out * out, axis=-1, keepdims=True)
    out_ref[...] = x_ref[...] + out * lax.rsqrt(ms + EPS) * npost_ref[...]


def _merge(x2, y_ssd, os_, lses, proj2, kv3, w_so, w_ao, w_mo, w_out, n_post, seq, tm):
    t = x2.shape[0]
    per_batch = seq // tm
    row = lambda i: (i, 0)
    const = lambda i: (0, 0)

    def pblk(j):
        return pl.BlockSpec((tm, CB), lambda i: (i, j))

    return pl.pallas_call(
        _merge_kernel,
        grid=(t // tm,),
        in_specs=[
            pl.BlockSpec((tm, D_MODEL), row),
            pl.BlockSpec((tm, D_INNER), row),
            pl.BlockSpec((tm, ATT_W), row), pl.BlockSpec((tm, ATT_W), row), pl.BlockSpec((tm, ATT_W), row),
            pl.BlockSpec((tm, ATT_W), row), pl.BlockSpec((tm, ATT_W), row), pl.BlockSpec((tm, ATT_W), row),
            pblk(BLK_ZATT), pblk(BLK_QMEM), pblk(BLK_ZMEM),
            pblk(BLK_GATE), pblk(BLK_GATE + 1), pblk(BLK_GATE + 2),
            pl.BlockSpec((1, MEM_LEN, 2 * MEM_W), lambda i: (i // per_batch, 0, 0)),
            pl.BlockSpec((D_INNER, D_MODEL), const),
            pl.BlockSpec((ATT_W, D_MODEL), const),
            pl.BlockSpec((MEM_W, D_MODEL), const),
            pl.BlockSpec((D_MODEL, D_MODEL), const),
            pl.BlockSpec((1, D_MODEL), const),
        ],
        out_specs=pl.BlockSpec((tm, D_MODEL), row),
        out_shape=jax.ShapeDtypeStruct((t, D_MODEL), F32),
        compiler_params=pltpu.CompilerParams(
            dimension_semantics=("parallel",), vmem_limit_bytes=VMEM_LIMIT),
        name="merge",
    )(x2, y_ssd, *os_, *lses, proj2, proj2, proj2, proj2, proj2, proj2, kv3,
      w_so, w_ao, w_mo, w_out, n_post)


def _pad_lanes(v, width):
    return jnp.pad(v, ((0, 0), (0, width - v.shape[-1])))


def kernel(x, mem, norm_pre, norm_post, w_in, conv_w, conv_b, dt_bias, a_log, d_skip, ssd_norm,
           w_ssd_out, w_attn_out, mem_norm, w_mem_kv, w_mem_out, w_out):
    bsz, seq, _ = x.shape
    depth = w_in.shape[0]
    t = bsz * seq
    tm_in = 1024
    tm_merge = 256
    assert seq % (ATT_BLOCK * ATT_PATTERNS[-1][1]) == 0 and t % tm_in == 0 and seq % tm_merge == 0

    x2 = x.reshape(t, D_MODEL)
    mem2 = mem.reshape(bsz * MEM_LEN, D_MODEL)
    for l in range(depth):
        w = w_in[l]
        w_main = jnp.concatenate([w[:, :OFF_DT], w[:, OFF_QKV:]], axis=1).astype(BF16)
        w_dt = _pad_lanes(w[:, OFF_DT:OFF_QKV], DT_PAD).astype(BF16)

        proj2, dt2 = _inproj(x2, norm_pre[l][None, :], w_main, w_dt, tm_in)
        proj3 = proj2.reshape(bsz, seq, N_MAIN)
        kv = _memkv(mem2, mem_norm[l][None, :], w_mem_kv[l].astype(BF16))

        y_ssd = _ssd(proj3, dt2.reshape(bsz, seq, DT_PAD), conv_w[l], conv_b[l][None, :],
                     _pad_lanes(dt_bias[l][None, :], DT_PAD), _pad_lanes(a_log[l][None, :], DT_PAD),
                     jnp.repeat(d_skip[l], SSD_HEAD_DIM)[None, :], ssd_norm[l][None, :])

        outs, lses = [], []
        for g, (_, dilation) in enumerate(ATT_PATTERNS):
            o, lse = _attention(proj3, g, dilation)
            outs.append(o)
            lses.append(lse)

        x2 = _merge(x2, y_ssd.reshape(t, D_INNER), outs, lses, proj2,
                    kv.reshape(bsz, MEM_LEN, 2 * MEM_W),
                    w_ssd_out[l].astype(BF16), w_attn_out[l].astype(BF16),
                    w_mem_out[l].astype(BF16), w_out[l].astype(BF16),
                    norm_post[l][None, :], seq, tm_merge)
    return x2.reshape(bsz, seq, D_MODEL)
```

```python
import jax
import jax.numpy as jnp
from jax import lax
from jax.experimental import pallas as pl
from jax.experimental.pallas import tpu as pltpu

F32 = jnp.float32
BF16 = jnp.bfloat16

EPS = 1e-6
D_MODEL = 1024

D_INNER = 2048
SSD_HEAD_DIM = 64
SSD_HEADS = 32
SSD_GROUPS = 8
D_STATE = 128
CONV_K = 4
CONV_DIM = D_INNER + 2 * SSD_GROUPS * D_STATE
CHUNK = 128

ATT_PATTERNS = ((128, 1), (512, 4), (2048, 16))
ATT_GROUPS = 3
ATT_HEADS = 16
ATT_HEAD_DIM = 64
ATT_W = 1024
ATT_BLOCK = 128

MEM_LEN = 256
MEM_HEADS = 4
MEM_HEAD_DIM = 256
MEM_W = 1024

OFF_XBC = D_INNER
OFF_DT = OFF_XBC + CONV_DIM
OFF_QKV = OFF_DT + SSD_HEADS
OFF_ZATT = OFF_QKV + 3 * ATT_GROUPS * ATT_W

CB = 1024
BLK_ZATT = 6
BLK_QMEM = 7
BLK_ZMEM = 8
BLK_GATE = 9
N_MAIN_BLK = 12
N_QKV_BLK = 3
N_BLK = N_MAIN_BLK + ATT_GROUPS * N_QKV_BLK

LANES = 128
N_SLAB = D_MODEL // LANES
DT_PAD = LANES

TM_IN = 1024
TM_MERGE = 256
WIN4 = 4 * ATT_BLOCK
WIN16 = TM_IN
ROWS16 = WIN16 // 16

VMEM_LIMIT = 56 * 1024 * 1024

NEG_INF = float("-inf")


def _sigmoid(v):
    return 1.0 / (1.0 + jnp.exp(-v))


def _silu(v):
    return v * _sigmoid(v)


def _split_dot(lhs_bf16, v):
    hi = v.astype(BF16)
    r1 = v - hi.astype(F32)
    mid = r1.astype(BF16)
    lo = (r1 - mid.astype(F32)).astype(BF16)
    acc = jnp.dot(lhs_bf16, hi, preferred_element_type=F32)
    acc = acc + jnp.dot(lhs_bf16, mid, preferred_element_type=F32)
    return acc + jnp.dot(lhs_bf16, lo, preferred_element_type=F32)


def _inproj_kernel(x_ref, g_ref, w_ref, wdt_ref, main_ref, q1_ref, q4_ref, q16_ref, dt_ref,
                   h_ref, hs_ref):
    j = pl.program_id(1)

    @pl.when(j == 0)
    def _():
        x = x_ref[...]
        ms = jnp.mean(x * x, axis=-1, keepdims=True)
        h = x * lax.rsqrt(ms + EPS) * g_ref[...]
        hb = h.astype(BF16)
        h_ref[0] = hb
        dt_ref[...] = jnp.dot(hb, wdt_ref[...], preferred_element_type=F32)
        for s in range(N_SLAB):
            sl = slice(s * LANES, (s + 1) * LANES)
            hs_ref[s] = h[:, sl]
            for w in range(TM_IN // WIN4):
                for r in range(4):
                    dst = w * WIN4 + r * ATT_BLOCK
                    h_ref[1, dst:dst + ATT_BLOCK, sl] = (
                        hs_ref[s, pl.ds(w * WIN4 + r, ATT_BLOCK, stride=4), :].astype(BF16))
            for r in range(16):
                h_ref[2, r * ROWS16:(r + 1) * ROWS16, sl] = (
                    hs_ref[s, pl.ds(r, ROWS16, stride=16), :].astype(BF16))

    def project(which, out_ref):
        out_ref[...] = jnp.dot(h_ref[which], w_ref[...], preferred_element_type=F32).astype(BF16)

    @pl.when(j < N_MAIN_BLK)
    def _():
        project(0, main_ref)

    @pl.when(jnp.logical_and(j >= N_MAIN_BLK, j < N_MAIN_BLK + N_QKV_BLK))
    def _():
        project(0, q1_ref)

    @pl.when(jnp.logical_and(j >= N_MAIN_BLK + N_QKV_BLK, j < N_MAIN_BLK + 2 * N_QKV_BLK))
    def _():
        project(1, q4_ref)

    @pl.when(j >= N_MAIN_BLK + 2 * N_QKV_BLK)
    def _():
        project(2, q16_ref)


def _inproj(x2, gain, w_all, w_dt):
    t = x2.shape[0]

    def qkv_map(first):
        return lambda i, j: (i, jnp.clip(j - first, 0, N_QKV_BLK - 1))

    qkv_shape = jax.ShapeDtypeStruct((t, N_QKV_BLK * CB), BF16)
    return pl.pallas_call(
        _inproj_kernel,
        grid=(t // TM_IN, N_BLK),
        in_specs=[
            pl.BlockSpec((TM_IN, D_MODEL), lambda i, j: (i, 0)),
            pl.BlockSpec((1, D_MODEL), lambda i, j: (0, 0)),
            pl.BlockSpec((D_MODEL, CB), lambda i, j: (0, j)),
            pl.BlockSpec((D_MODEL, DT_PAD), lambda i, j: (0, 0)),
        ],
        out_specs=[
            pl.BlockSpec((TM_IN, CB), lambda i, j: (i, jnp.minimum(j, N_MAIN_BLK - 1))),
            pl.BlockSpec((TM_IN, CB), qkv_map(N_MAIN_BLK)),
            pl.BlockSpec((TM_IN, CB), qkv_map(N_MAIN_BLK + N_QKV_BLK)),
            pl.BlockSpec((TM_IN, CB), qkv_map(N_MAIN_BLK + 2 * N_QKV_BLK)),
            pl.BlockSpec((TM_IN, DT_PAD), lambda i, j: (i, 0)),
        ],
        out_shape=[
            jax.ShapeDtypeStruct((t, N_MAIN_BLK * CB), BF16),
            qkv_shape, qkv_shape, qkv_shape,
            jax.ShapeDtypeStruct((t, DT_PAD), F32),
        ],
        scratch_shapes=[
            pltpu.VMEM((3, TM_IN, D_MODEL), BF16),
            pltpu.VMEM((N_SLAB, TM_IN, LANES), F32),
        ],
        compiler_params=pltpu.CompilerParams(
            dimension_semantics=("parallel", "arbitrary"), vmem_limit_bytes=VMEM_LIMIT),
        name="inproj",
    )(x2, gain, w_all, w_dt)


def _memkv_kernel(m_ref, g_ref, w_ref, out_ref):
    m = m_ref[...]
    ms = jnp.mean(m * m, axis=-1, keepdims=True)
    h = (m * lax.rsqrt(ms + EPS) * g_ref[...]).astype(BF16)
    out_ref[...] = jnp.dot(h, w_ref[...], preferred_element_type=F32).astype(BF16)


def _memkv(mem2, gain, w_kv):
    rows = mem2.shape[0]
    return pl.pallas_call(
        _memkv_kernel,
        grid=(2 * MEM_W // CB,),
        in_specs=[
            pl.BlockSpec((rows, D_MODEL), lambda j: (0, 0)),
            pl.BlockSpec((1, D_MODEL), lambda j: (0, 0)),
            pl.BlockSpec((D_MODEL, CB), lambda j: (0, j)),
        ],
        out_specs=pl.BlockSpec((rows, CB), lambda j: (0, j)),
        out_shape=jax.ShapeDtypeStruct((rows, 2 * MEM_W), BF16),
        compiler_params=pltpu.CompilerParams(
            dimension_semantics=("arbitrary",), vmem_limit_bytes=VMEM_LIMIT),
        name="memkv",
    )(mem2, gain, w_kv)


N_PAIR = SSD_HEADS // 2
TAIL = 8


def _ssd_kernel(z_ref, x_ref, bc_ref, dt_ref, cw_ref, cb_ref, dtb_ref, alog_ref, dskip_ref,
                nrm_ref, y_ref, state_ref, cbuf_ref, act_ref):
    @pl.when(pl.program_id(1) == 0)
    def _():
        state_ref[...] = jnp.zeros_like(state_ref)
        cbuf_ref[0:TAIL, :] = jnp.zeros((TAIL, CONV_DIM), F32)

    cbuf_ref[TAIL:TAIL + CHUNK, 0:D_INNER] = x_ref[0].astype(F32)
    cbuf_ref[TAIL:TAIL + CHUNK, D_INNER:CONV_DIM] = bc_ref[0].astype(F32)
    conv = cb_ref[...]
    for k in range(CONV_K):
        lo = TAIL - (CONV_K - 1) + k
        conv = conv + cw_ref[k:k + 1, :] * cbuf_ref[lo:lo + CHUNK, :]
    cbuf_ref[0:TAIL, :] = cbuf_ref[CHUNK:CHUNK + TAIL, :]
    act_ref[...] = _silu(conv)

    v = dt_ref[0] + dtb_ref[...]
    dt = jnp.maximum(v, 0.0) + jnp.log1p(jnp.exp(-jnp.abs(v)))
    a = -jnp.exp(alog_ref[...])
    row = lax.broadcasted_iota(jnp.int32, (CHUNK, CHUNK), 0)
    col = lax.broadcasted_iota(jnp.int32, (CHUNK, CHUNK), 1)
    causal = row >= col
    tri = jnp.where(causal, 1.0, 0.0).astype(BF16)
    acum = _split_dot(tri, dt * a)
    total = acum[CHUNK - 1:CHUNK, :]
    acum_t = acum.T
    dt_t = dt.T
    dsts_t = (jnp.exp(total - acum) * dt).T
    cdec = jnp.exp(total)

    lane = lax.broadcasted_iota(jnp.int32, (CHUNK, LANES), 1)
    first = lane < SSD_HEAD_DIM

    for g in range(SSD_GROUPS):
        b_f = act_ref[:, D_INNER + g * D_STATE:D_INNER + (g + 1) * D_STATE]
        c_f = act_ref[:, D_INNER + (SSD_GROUPS + g) * D_STATE:D_INNER + (SSD_GROUPS + g + 1) * D_STATE]
        b_t = b_f.T
        cbm = lax.dot_general(c_f.astype(BF16), b_f.astype(BF16), (((1,), (1,)), ((), ())),
                              preferred_element_type=F32)
        y_pairs = []
        for pp in range(2):
            pair = 2 * g + pp
            x_pair = act_ref[:, pair * LANES:(pair + 1) * LANES]
            s_prev = state_ref[pair]
            rhs = jnp.concatenate([x_pair, s_prev], axis=0).astype(BF16)
            x_b = x_pair.astype(BF16)
            ys, sts, decs = [], [], []
            for e in range(2):
                h = 2 * pair + e
                acb = jnp.broadcast_to(acum[:, h:h + 1], (CHUNK, CHUNK))
                seg = acb - acum_t[h:h + 1, :]
                decay = jnp.exp(jnp.where(causal, seg, NEG_INF))
                m = cbm * decay * dt_t[h:h + 1, :]
                c_exp = c_f * jnp.exp(acb)
                lhs = jnp.concatenate([m, c_exp], axis=1).astype(BF16)
                ys.append(jnp.dot(lhs, rhs, preferred_element_type=F32))
                bts = (b_t * dsts_t[h:h + 1, :]).astype(BF16)
                sts.append(jnp.dot(bts, x_b, preferred_element_type=F32))
                decs.append(jnp.broadcast_to(cdec[:, h:h + 1], (CHUNK, LANES)))
            state_ref[pair] = (s_prev * jnp.where(first, decs[0], decs[1])
                               + jnp.where(first, sts[0], sts[1]))
            y_pair = jnp.where(first, ys[0], ys[1])
            y_pairs.append(y_pair + dskip_ref[:, pair * LANES:(pair + 1) * LANES] * x_pair)

        gsl = slice(2 * g * LANES, (2 * g + 2) * LANES)
        yg = jnp.concatenate(y_pairs, axis=1) * _silu(z_ref[0, :, gsl].astype(F32))
        ms = jnp.mean(yg * yg, axis=-1, keepdims=True)
        y_ref[0, :, gsl] = (yg * lax.rsqrt(ms + EPS) * nrm_ref[:, gsl]).astype(BF16)


def _ssd(main3, dt3, conv_w, conv_b, dt_bias, a_log, d_skip_x, ssd_norm):
    b, s, _ = main3.shape
    nc = s // CHUNK
    wide = D_INNER
    return pl.pallas_call(
        _ssd_kernel,
        grid=(b, nc),
        in_specs=[
            pl.BlockSpec((1, CHUNK, wide), lambda i, c: (i, c, 0)),
            pl.BlockSpec((1, CHUNK, wide), lambda i, c: (i, c, 1)),
            pl.BlockSpec((1, CHUNK, wide), lambda i, c: (i, c, 2)),
            pl.BlockSpec((1, CHUNK, DT_PAD), lambda i, c: (i, c, 0)),
            pl.BlockSpec((CONV_K, CONV_DIM), lambda i, c: (0, 0)),
            pl.BlockSpec((1, CONV_DIM), lambda i, c: (0, 0)),
            pl.BlockSpec((1, DT_PAD), lambda i, c: (0, 0)),
            pl.BlockSpec((1, DT_PAD), lambda i, c: (0, 0)),
            pl.BlockSpec((1, D_INNER), lambda i, c: (0, 0)),
            pl.BlockSpec((1, D_INNER), lambda i, c: (0, 0)),
        ],
        out_specs=pl.BlockSpec((1, CHUNK, D_INNER), lambda i, c: (i, c, 0)),
        out_shape=jax.ShapeDtypeStruct((b, s, D_INNER), BF16),
        scratch_shapes=[
            pltpu.VMEM((N_PAIR, D_STATE, LANES), F32),
            pltpu.VMEM((TAIL + CHUNK, CONV_DIM), F32),
            pltpu.VMEM((CHUNK, CONV_DIM), F32),
        ],
        compiler_params=pltpu.CompilerParams(
            dimension_semantics=("parallel", "arbitrary"), vmem_limit_bytes=VMEM_LIMIT),
        name="ssd",
    )(main3, main3, main3, dt3, conv_w, conv_b, dt_bias, a_log, d_skip_x, ssd_norm)


def _make_attn_kernel(parts):
    rows = ATT_BLOCK // parts

    def load(ref, sl):
        if parts == 1:
            return ref[:, sl]
        return jnp.concatenate([ref[p, :, sl] for p in range(parts)], axis=0)

    def store(ref, sl, val):
        if parts == 1:
            ref[:, sl] = val
        else:
            for p in range(parts):
                ref[p, :, sl] = val[p * rows:(p + 1) * rows]

    def attn_kernel(q_ref, k_ref, v_ref, o_ref, lse_ref, kp_ref, vp_ref):
        n = pl.program_id(2)

        @pl.when(n == 0)
        def _():
            kp_ref[...] = jnp.zeros_like(kp_ref)
            vp_ref[...] = jnp.zeros_like(vp_ref)

        qi = lax.broadcasted_iota(jnp.int32, (ATT_BLOCK, 2 * ATT_BLOCK), 0)
        kj = lax.broadcasted_iota(jnp.int32, (ATT_BLOCK, 2 * ATT_BLOCK), 1)
        band = jnp.logical_and(kj >= qi, kj - ATT_BLOCK <= qi)
        first_key = jnp.where(n > 0, 0, ATT_BLOCK)
        valid = jnp.logical_and(band, kj >= first_key)
        lane = lax.broadcasted_iota(jnp.int32, (ATT_BLOCK, LANES), 1)
        first = lane < ATT_HEAD_DIM

        for hp in range(ATT_HEADS // 2):
            sl = slice(hp * LANES, (hp + 1) * LANES)
            q2 = load(q_ref, sl) * (ATT_HEAD_DIM ** -0.5)
            k_cur = load(k_ref, sl)
            v_cur = load(v_ref, sl)
            kk = jnp.concatenate([kp_ref[:, sl], k_cur], axis=0)
            vv = jnp.concatenate([vp_ref[:, sl], v_cur], axis=0)
            outs, lses = [], []
            for e in range(2):
                qh = jnp.where(first if e == 0 else jnp.logical_not(first), q2, jnp.zeros_like(q2))
                sc = lax.dot_general(qh, kk, (((1,), (1,)), ((), ())), preferred_element_type=F32)
                sc = jnp.where(valid, sc, NEG_INF)
                mx = jnp.max(sc, axis=-1, keepdims=True)
                p = jnp.exp(sc - mx)
                den = jnp.sum(p, axis=-1, keepdims=True)
                o = jnp.dot(p.astype(BF16), vv, preferred_element_type=F32)
                outs.append(o / den)
                lses.append(jnp.broadcast_to(mx + jnp.log(den), (ATT_BLOCK, LANES)))
            store(o_ref, sl, jnp.where(first, outs[0], outs[1]).astype(BF16))
            store(lse_ref, sl, jnp.where(first, lses[0], lses[1]))
            kp_ref[:, sl] = k_cur
            vp_ref[:, sl] = v_cur

    return attn_kernel


def _attention(qkv2, bsz, seq, dilation):
    if dilation == 1:
        lead = (bsz, seq)
        blk = (None, ATT_BLOCK)
        pos = lambda i, r, n: (i, n)
        parts = 1
    elif dilation == 4:
        lead = (bsz, seq // WIN4, 4, ATT_BLOCK)
        blk = (None, None, None, ATT_BLOCK)
        pos = lambda i, r, n: (i, n, r, 0)
        parts = 1
    else:
        parts = ATT_BLOCK // ROWS16
        lead = (bsz, seq // WIN16, 16, ROWS16)
        blk = (None, parts, None, ROWS16)
        pos = lambda i, r, n: (i, n, r, 0)
    view = qkv2.reshape(lead + (N_QKV_BLK * CB,))
    nb = seq // dilation // ATT_BLOCK

    def in_spec(kind):
        return pl.BlockSpec(blk + (ATT_W,), lambda i, r, n: pos(i, r, n) + (kind,))

    out_spec = pl.BlockSpec(blk + (ATT_W,), lambda i, r, n: pos(i, r, n) + (0,))
    return pl.pallas_call(
        _make_attn_kernel(parts),
        grid=(bsz, dilation, nb),
        in_specs=[in_spec(kind) for kind in range(3)],
        out_specs=[out_spec, out_spec],
        out_shape=[
            jax.ShapeDtypeStruct(lead + (ATT_W,), BF16),
            jax.ShapeDtypeStruct(lead + (ATT_W,), F32),
        ],
        scratch_shapes=[
            pltpu.VMEM((ATT_BLOCK, ATT_W), BF16),
            pltpu.VMEM((ATT_BLOCK, ATT_W), BF16),
        ],
        compiler_params=pltpu.CompilerParams(
            dimension_semantics=("parallel", "parallel", "arbitrary"), vmem_limit_bytes=VMEM_LIMIT),
        name=f"attn_d{dilation}",
    )(view, view, view)


def _merge_kernel(x_ref, yssd_ref, o1_ref, o4_ref, o16_ref, l1_ref, l4_ref, l16_ref,
                  zatt_ref, qmem_ref, zmem_ref, g0_ref, g1_ref, g2_ref, kv_ref,
                  wso_ref, wao_ref, wmo_ref, wout_ref, npost_ref, out_ref, tok_ref):
    ssd = jnp.dot(yssd_ref[...], wso_ref[...], preferred_element_type=F32)
    merged = _sigmoid(g0_ref[...].astype(F32)) * ssd

    n4, n16 = TM_MERGE // 4, TM_MERGE // 16
    ys = []
    for s in range(N_SLAB):
        sl = slice(s * LANES, (s + 1) * LANES)
        for r in range(4):
            tok_ref[0, pl.ds(r, n4, stride=4), :] = o4_ref[r, :, sl].astype(F32)
            tok_ref[1, pl.ds(r, n4, stride=4), :] = l4_ref[r, :, sl]
        for r in range(16):
            tok_ref[2, pl.ds(r, n16, stride=16), :] = o16_ref[r, :, sl].astype(F32)
            tok_ref[3, pl.ds(r, n16, stride=16), :] = l16_ref[r, :, sl]
        la, lb, lc = l1_ref[:, sl], tok_ref[1], tok_ref[3]
        top = jnp.maximum(la, jnp.maximum(lb, lc))
        ea, eb, ec = jnp.exp(la - top), jnp.exp(lb - top), jnp.exp(lc - top)
        o = (ea * o1_ref[:, sl].astype(F32) + eb * tok_ref[0] + ec * tok_ref[2]) / (ea + eb + ec)
        ys.append((o * _silu(zatt_ref[:, sl].astype(F32))).astype(BF16))
    y_att = jnp.concatenate(ys, axis=1)
    att = jnp.dot(y_att, wao_ref[...], preferred_element_type=F32)
    merged = merged + _sigmoid(g1_ref[...].astype(F32)) * att

    heads = []
    for h in range(MEM_HEADS):
        sl = slice(h * MEM_HEAD_DIM, (h + 1) * MEM_HEAD_DIM)
        q = qmem_ref[:, sl] * (MEM_HEAD_DIM ** -0.5)
        k = kv_ref[0, :, sl]
        vv = kv_ref[0, :, MEM_W + h * MEM_HEAD_DIM:MEM_W + (h + 1) * MEM_HEAD_DIM]
        sc = lax.dot_general(q, k, (((1,), (1,)), ((), ())), preferred_element_type=F32)
        mx = jnp.max(sc, axis=-1, keepdims=True)
        p = jnp.exp(sc - mx)
        den = jnp.sum(p, axis=-1, keepdims=True)
        heads.append(jnp.dot(p.astype(BF16), vv, preferred_element_type=F32) / den)
    o_mem = jnp.concatenate(heads, axis=1)
    y_mem = (o_mem * _silu(zmem_ref[...].astype(F32))).astype(BF16)
    mem = jnp.dot(y_mem, wmo_ref[...], preferred_element_type=F32)
    merged = merged + _sigmoid(g2_ref[...].astype(F32)) * mem

    out = jnp.dot(merged.astype(BF16), wout_ref[...], preferred_element_type=F32)
    ms = jnp.mean(out * out, axis=-1, keepdims=True)
    out_ref[...] = x_ref[...] + out * lax.rsqrt(ms + EPS) * npost_ref[...]


def _merge(x2, y_ssd, o1, l1, o4, l4, o16, l16, main2, kv3, w_so, w_ao, w_mo, w_out, n_post, seq):
    t = x2.shape[0]
    tm = TM_MERGE
    per_batch = seq // tm
    row = lambda i: (i, 0)
    const = lambda i: (0, 0)

    def mblk(j):
        return pl.BlockSpec((tm, CB), lambda i: (i, j))

    per_win4, per_win16 = WIN4 // tm, WIN16 // tm
    spec4 = pl.BlockSpec((None, None, 4, tm // 4, ATT_W),
                         lambda i: (i // per_batch, (i % per_batch) // per_win4, 0, i % per_win4, 0))
    spec16 = pl.BlockSpec((None, None, 16, tm // 16, ATT_W),
                          lambda i: (i // per_batch, (i % per_batch) // per_win16, 0, i % per_win16, 0))
    return pl.pallas_call(
        _merge_kernel,
        grid=(t // tm,),
        in_specs=[
            pl.BlockSpec((tm, D_MODEL), row),
            pl.BlockSpec((tm, D_INNER), row),
            pl.BlockSpec((tm, ATT_W), row), spec4, spec16,
            pl.BlockSpec((tm, ATT_W), row), spec4, spec16,
            mblk(BLK_ZATT), mblk(BLK_QMEM), mblk(BLK_ZMEM),
            mblk(BLK_GATE), mblk(BLK_GATE + 1), mblk(BLK_GATE + 2),
            pl.BlockSpec((1, MEM_LEN, 2 * MEM_W), lambda i: (i // per_batch, 0, 0)),
            pl.BlockSpec((D_INNER, D_MODEL), const),
            pl.BlockSpec((ATT_W, D_MODEL), const),
            pl.BlockSpec((MEM_W, D_MODEL), const),
            pl.BlockSpec((D_MODEL, D_MODEL), const),
            pl.BlockSpec((1, D_MODEL), const),
        ],
        out_specs=pl.BlockSpec((tm, D_MODEL), row),
        out_shape=jax.ShapeDtypeStruct((t, D_MODEL), F32),
        scratch_shapes=[pltpu.VMEM((4, tm, LANES), F32)],
        compiler_params=pltpu.CompilerParams(
            dimension_semantics=("parallel",), vmem_limit_bytes=VMEM_LIMIT),
        name="merge",
    )(x2, y_ssd, o1, o4, o16, l1, l4, l16, main2, main2, main2, main2, main2, main2, kv3,
      w_so, w_ao, w_mo, w_out, n_post)


def _pad_lanes(v, width):
    return jnp.pad(v, ((0, 0), (0, width - v.shape[-1])))


def _regroup_w_in(w):
    cols = [w[:, :OFF_DT], w[:, OFF_ZATT:]]
    for g in range(ATT_GROUPS):
        for kind in range(3):
            start = OFF_QKV + (kind * ATT_GROUPS + g) * ATT_W
            cols.append(w[:, start:start + ATT_W])
    return jnp.concatenate(cols, axis=1).astype(BF16)


def kernel(x, mem, norm_pre, norm_post, w_in, conv_w, conv_b, dt_bias, a_log, d_skip, ssd_norm,
           w_ssd_out, w_attn_out, mem_norm, w_mem_kv, w_mem_out, w_out):
    bsz, seq, _ = x.shape
    depth = w_in.shape[0]
    t = bsz * seq
    assert seq % (16 * ATT_BLOCK) == 0 and seq % TM_IN == 0 and WIN4 % TM_MERGE == 0

    x2 = x.reshape(t, D_MODEL)
    mem2 = mem.reshape(bsz * MEM_LEN, D_MODEL)
    for l in range(depth):
        w = w_in[l]
        w_dt = _pad_lanes(w[:, OFF_DT:OFF_QKV], DT_PAD).astype(BF16)
        main2, qkv1, qkv4, qkv16, dt2 = _inproj(x2, norm_pre[l][None, :], _regroup_w_in(w), w_dt)
        main3 = main2.reshape(bsz, seq, N_MAIN_BLK * CB)
        kv = _memkv(mem2, mem_norm[l][None, :], w_mem_kv[l].astype(BF16))

        y_ssd = _ssd(main3, dt2.reshape(bsz, seq, DT_PAD), conv_w[l], conv_b[l][None, :],
                     _pad_lanes(dt_bias[l][None, :], DT_PAD), _pad_lanes(a_log[l][None, :], DT_PAD),
                     jnp.repeat(d_skip[l], SSD_HEAD_DIM)[None, :], ssd_norm[l][None, :])

        o1, l1 = _attention(qkv1, bsz, seq, 1)
        o4, l4 = _attention(qkv4, bsz, seq, 4)
        o16, l16 = _attention(qkv16, bsz, seq, 16)

        x2 = _merge(x2, y_ssd.reshape(t, D_INNER), o1.reshape(t, ATT_W), l1.reshape(t, ATT_W),
                    o4, l4, o16, l16, main2, kv.reshape(bsz, MEM_LEN, 2 * MEM_W),
                    w_ssd_out[l].astype(BF16), w_attn_out[l].astype(BF16),
                    w_mem_out[l].astype(BF16), w_out[l].astype(BF16), norm_post[l][None, :], seq)
    return x2.reshape(bsz, seq, D_MODEL)
```

```python
import math

import jax
import jax.numpy as jnp
from jax import lax
from jax.experimental import pallas as pl
from jax.experimental.pallas import tpu as pltpu

F32 = jnp.float32
BF16 = jnp.bfloat16

EPS = 1e-6
D_MODEL = 1024

D_INNER = 2048
SSD_HEAD_DIM = 64
SSD_HEADS = 32
SSD_GROUPS = 8
D_STATE = 128
CONV_K = 4
CONV_DIM = D_INNER + 2 * SSD_GROUPS * D_STATE
CHUNK = 128

ATT_DILATIONS = (1, 4, 16)
ATT_GROUPS = 3
ATT_HEADS = 16
ATT_HEAD_DIM = 64
ATT_W = 1024
ATT_BLOCK = 128

MEM_LEN = 256
MEM_HEADS = 4
MEM_HEAD_DIM = 256
MEM_W = 1024

OFF_XBC = D_INNER
OFF_DT = OFF_XBC + CONV_DIM
OFF_QKV = OFF_DT + SSD_HEADS
OFF_ZATT = OFF_QKV + 3 * ATT_GROUPS * ATT_W

CB = 1024
BLK_ZATT = 6
BLK_QMEM = 7
BLK_ZMEM = 8
BLK_GATE = 9
N_MAIN = 12 * CB
N_QKV = 3 * ATT_GROUPS * ATT_W
CB_IN = 1536
MAIN_STEPS = N_MAIN // CB_IN
GROUP_STEPS = 3 * ATT_W // CB_IN
N_STEPS = MAIN_STEPS + ATT_GROUPS * GROUP_STEPS

LANES = 128
N_SLAB = D_MODEL // LANES
DT_PAD = LANES

TM_IN = 1024
TM_MERGE = 256
WIN = {1: ATT_BLOCK, 4: 4 * ATT_BLOCK, 16: TM_IN}
Q_STEP = 4 * ATT_BLOCK

VMEM_LIMIT = 56 * 1024 * 1024

NEG_INF = float("-inf")
LOG2E = math.log2(math.e)


def _sigmoid(v):
    return 0.5 * jnp.tanh(0.5 * v) + 0.5


def _silu(v):
    return v * _sigmoid(v)


def _split_dot(lhs_bf16, v):
    hi = v.astype(BF16)
    r1 = v - hi.astype(F32)
    mid = r1.astype(BF16)
    lo = (r1 - mid.astype(F32)).astype(BF16)
    acc = jnp.dot(lhs_bf16, hi, preferred_element_type=F32)
    acc = acc + jnp.dot(lhs_bf16, mid, preferred_element_type=F32)
    return acc + jnp.dot(lhs_bf16, lo, preferred_element_type=F32)


def _inproj_kernel(x_ref, g_ref, w_ref, wdt_ref, main_ref, qkv_ref, dt_ref, h_ref, hs_ref):
    j = pl.program_id(1)

    @pl.when(j == 0)
    def _():
        x = x_ref[...]
        ms = jnp.mean(x * x, axis=-1, keepdims=True)
        h = x * lax.rsqrt(ms + EPS) * g_ref[...]
        hb = h.astype(BF16)
        h_ref[0] = hb
        dt_ref[...] = jnp.dot(hb, wdt_ref[...], preferred_element_type=F32)
        rows16 = TM_IN // 16
        for s in range(N_SLAB):
            sl = slice(s * LANES, (s + 1) * LANES)
            hs_ref[s] = h[:, sl]
            for w in range(TM_IN // WIN[4]):
                for r in range(4):
                    dst = w * WIN[4] + r * ATT_BLOCK
                    h_ref[1, dst:dst + ATT_BLOCK, sl] = (
                        hs_ref[s, pl.ds(w * WIN[4] + r, ATT_BLOCK, stride=4), :].astype(BF16))
            for r in range(16):
                h_ref[2, r * rows16:(r + 1) * rows16, sl] = (
                    hs_ref[s, pl.ds(r, rows16, stride=16), :].astype(BF16))

    @pl.when(j < MAIN_STEPS)
    def _():
        main_ref[...] = jnp.dot(h_ref[0], w_ref[...], preferred_element_type=F32).astype(BF16)

    @pl.when(j >= MAIN_STEPS)
    def _():
        group = (j - MAIN_STEPS) // GROUP_STEPS
        qkv_ref[...] = jnp.dot(h_ref[group], w_ref[...], preferred_element_type=F32).astype(BF16)


def _inproj(x2, gain, w_all, w_dt):
    t = x2.shape[0]
    return pl.pallas_call(
        _inproj_kernel,
        grid=(t // TM_IN, N_STEPS),
        in_specs=[
            pl.BlockSpec((TM_IN, D_MODEL), lambda i, j: (i, 0)),
            pl.BlockSpec((1, D_MODEL), lambda i, j: (0, 0)),
            pl.BlockSpec((D_MODEL, CB_IN), lambda i, j: (0, j)),
            pl.BlockSpec((D_MODEL, DT_PAD), lambda i, j: (0, 0)),
        ],
        out_specs=[
            pl.BlockSpec((TM_IN, CB_IN), lambda i, j: (i, jnp.minimum(j, MAIN_STEPS - 1))),
            pl.BlockSpec((TM_IN, CB_IN), lambda i, j: (i, jnp.maximum(j - MAIN_STEPS, 0))),
            pl.BlockSpec((TM_IN, DT_PAD), lambda i, j: (i, 0)),
        ],
        out_shape=[
            jax.ShapeDtypeStruct((t, N_MAIN), BF16),
            jax.ShapeDtypeStruct((t, N_QKV), BF16),
            jax.ShapeDtypeStruct((t, DT_PAD), F32),
        ],
        scratch_shapes=[
            pltpu.VMEM((3, TM_IN, D_MODEL), BF16),
            pltpu.VMEM((N_SLAB, TM_IN, LANES), F32),
        ],
        compiler_params=pltpu.CompilerParams(
            dimension_semantics=("parallel", "arbitrary"), vmem_limit_bytes=VMEM_LIMIT),
        name="inproj",
    )(x2, gain, w_all, w_dt)


def _memkv_kernel(m_ref, g_ref, w_ref, out_ref):
    m = m_ref[...]
    ms = jnp.mean(m * m, axis=-1, keepdims=True)
    h = (m * lax.rsqrt(ms + EPS) * g_ref[...]).astype(BF16)
    out_ref[...] = jnp.dot(h, w_ref[...], preferred_element_type=F32).astype(BF16)


def _memkv(mem2, gain, w_kv):
    rows = mem2.shape[0]
    return pl.pallas_call(
        _memkv_kernel,
        grid=(2 * MEM_W // CB,),
        in_specs=[
            pl.BlockSpec((rows, D_MODEL), lambda j: (0, 0)),
            pl.BlockSpec((1, D_MODEL), lambda j: (0, 0)),
            pl.BlockSpec((D_MODEL, CB), lambda j: (0, j)),
        ],
        out_specs=pl.BlockSpec((rows, CB), lambda j: (0, j)),
        out_shape=jax.ShapeDtypeStruct((rows, 2 * MEM_W), BF16),
        compiler_params=pltpu.CompilerParams(
            dimension_semantics=("arbitrary",), vmem_limit_bytes=VMEM_LIMIT),
        name="memkv",
    )(mem2, gain, w_kv)


N_PAIR = SSD_HEADS // 2
TAIL = 8
CONV_SLABS = CONV_DIM // LANES


def _ssd_kernel(z_ref, x_ref, bc_ref, dt_ref, cw_ref, cb_ref, dtb_ref, alog_ref, dskip_ref,
                nrm_ref, y_ref, state_ref, cbuf_ref, act_ref):
    @pl.when(pl.program_id(1) == 0)
    def _():
        state_ref[...] = jnp.zeros_like(state_ref)
        cbuf_ref[:, 0:TAIL, :] = jnp.zeros((CONV_SLABS, TAIL, LANES), F32)

    for s in range(CONV_SLABS):
        sl = slice(s * LANES, (s + 1) * LANES)
        src = x_ref if s < D_INNER // LANES else bc_ref
        off = (s * LANES) % D_INNER
        cbuf_ref[s, TAIL:TAIL + CHUNK, :] = src[0, :, off:off + LANES].astype(F32)
        conv = cb_ref[:, sl]
        for k in range(CONV_K):
            lo = TAIL - (CONV_K - 1) + k
            conv = conv + cw_ref[k:k + 1, sl] * cbuf_ref[s, pl.ds(lo, CHUNK, stride=1), :]
        cbuf_ref[s, 0:TAIL, :] = cbuf_ref[s, CHUNK:CHUNK + TAIL, :]
        act_ref[:, sl] = _silu(conv)

    v = dt_ref[0] + dtb_ref[...]
    dt = jnp.maximum(v, 0.0) + jnp.log1p(jnp.exp(-jnp.abs(v)))
    a2 = -jnp.exp(alog_ref[...]) * LOG2E
    row = lax.broadcasted_iota(jnp.int32, (CHUNK, CHUNK), 0)
    col = lax.broadcasted_iota(jnp.int32, (CHUNK, CHUNK), 1)
    causal = row >= col
    tri = jnp.where(causal, 1.0, 0.0).astype(BF16)
    acum = _split_dot(tri, dt * a2)
    total = acum[CHUNK - 1:CHUNK, :]
    acum_t = acum.T
    dt_t = dt.T
    dsts_t = (jnp.exp2(total - acum) * dt).T
    cdec = jnp.exp2(total)

    lane = lax.broadcasted_iota(jnp.int32, (CHUNK, LANES), 1)
    first = lane < SSD_HEAD_DIM

    for g in range(SSD_GROUPS):
        b_f = act_ref[:, D_INNER + g * D_STATE:D_INNER + (g + 1) * D_STATE]
        c_f = act_ref[:, D_INNER + (SSD_GROUPS + g) * D_STATE:D_INNER + (SSD_GROUPS + g + 1) * D_STATE]
        b_t = b_f.T
        cbm = lax.dot_general(c_f.astype(BF16), b_f.astype(BF16), (((1,), (1,)), ((), ())),
                              preferred_element_type=F32)
        y_pairs = []
        for pp in range(2):
            pair = 2 * g + pp
            x_pair = act_ref[:, pair * LANES:(pair + 1) * LANES]
            s_prev = state_ref[pair]
            rhs = jnp.concatenate([x_pair, s_prev], axis=0).astype(BF16)
            x_b = x_pair.astype(BF16)
            ys, sts, decs = [], [], []
            for e in range(2):
                h = 2 * pair + e
                acb = jnp.broadcast_to(acum[:, h:h + 1], (CHUNK, CHUNK))
                seg = acb - acum_t[h:h + 1, :]
                decay = jnp.exp2(jnp.where(causal, seg, NEG_INF))
                m = cbm * decay * dt_t[h:h + 1, :]
                c_exp = c_f * jnp.exp2(acb)
                lhs = jnp.concatenate([m, c_exp], axis=1).astype(BF16)
                ys.append(jnp.dot(lhs, rhs, preferred_element_type=F32))
                bts = (b_t * dsts_t[h:h + 1, :]).astype(BF16)
                sts.append(jnp.dot(bts, x_b, preferred_element_type=F32))
                decs.append(jnp.broadcast_to(cdec[:, h:h + 1], (CHUNK, LANES)))
            state_ref[pair] = (s_prev * jnp.where(first, decs[0], decs[1])
                               + jnp.where(first, sts[0], sts[1]))
            y_pair = jnp.where(first, ys[0], ys[1])
            y_pairs.append(y_pair + dskip_ref[:, pair * LANES:(pair + 1) * LANES] * x_pair)

        gsl = slice(2 * g * LANES, (2 * g + 2) * LANES)
        yg = jnp.concatenate(y_pairs, axis=1) * _silu(z_ref[0, :, gsl].astype(F32))
        ms = jnp.mean(yg * yg, axis=-1, keepdims=True)
        y_ref[0, :, gsl] = (yg * lax.rsqrt(ms + EPS) * nrm_ref[:, gsl]).astype(BF16)


def _ssd(main3, dt3, conv_w, conv_b, dt_bias, a_log, d_skip_x, ssd_norm):
    b, s, _ = main3.shape
    nc = s // CHUNK
    wide = D_INNER
    return pl.pallas_call(
        _ssd_kernel,
        grid=(b, nc),
        in_specs=[
            pl.BlockSpec((1, CHUNK, wide), lambda i, c: (i, c, 0)),
            pl.BlockSpec((1, CHUNK, wide), lambda i, c: (i, c, 1)),
            pl.BlockSpec((1, CHUNK, wide), lambda i, c: (i, c, 2)),
            pl.BlockSpec((1, CHUNK, DT_PAD), lambda i, c: (i, c, 0)),
            pl.BlockSpec((CONV_K, CONV_DIM), lambda i, c: (0, 0)),
            pl.BlockSpec((1, CONV_DIM), lambda i, c: (0, 0)),
            pl.BlockSpec((1, DT_PAD), lambda i, c: (0, 0)),
            pl.BlockSpec((1, DT_PAD), lambda i, c: (0, 0)),
            pl.BlockSpec((1, D_INNER), lambda i, c: (0, 0)),
            pl.BlockSpec((1, D_INNER), lambda i, c: (0, 0)),
        ],
        out_specs=pl.BlockSpec((1, CHUNK, D_INNER), lambda i, c: (i, c, 0)),
        out_shape=jax.ShapeDtypeStruct((b, s, D_INNER), BF16),
        scratch_shapes=[
            pltpu.VMEM((N_PAIR, D_STATE, LANES), F32),
            pltpu.VMEM((CONV_SLABS, TAIL + CHUNK, LANES), F32),
            pltpu.VMEM((CHUNK, CONV_DIM), F32),
        ],
        compiler_params=pltpu.CompilerParams(
            dimension_semantics=("parallel", "arbitrary"), vmem_limit_bytes=VMEM_LIMIT),
        name="ssd",
    )(main3, main3, main3, dt3, conv_w, conv_b, dt_bias, a_log, d_skip_x, ssd_norm)


def _make_attn_kernel(rows):
    parts = ATT_BLOCK // rows
    blocks = Q_STEP // ATT_BLOCK

    def load(ref, blk, sl):
        pieces = [ref[blk * parts + p, :, sl] for p in range(parts)]
        return pieces[0] if parts == 1 else jnp.concatenate(pieces, axis=0)

    def store(ref, blk, sl, val):
        for p in range(parts):
            ref[blk * parts + p, :, sl] = val[p * rows:(p + 1) * rows]

    def attn_kernel(q_ref, k_ref, v_ref, o_ref, lse_ref, kp_ref, vp_ref):
        n = pl.program_id(2)

        @pl.when(n == 0)
        def _():
            kp_ref[...] = jnp.zeros_like(kp_ref)
            vp_ref[...] = jnp.zeros_like(vp_ref)

        qi = lax.broadcasted_iota(jnp.int32, (ATT_BLOCK, 2 * ATT_BLOCK), 0)
        kj = lax.broadcasted_iota(jnp.int32, (ATT_BLOCK, 2 * ATT_BLOCK), 1)
        band = jnp.logical_and(kj >= qi, kj - ATT_BLOCK <= qi)
        lane = lax.broadcasted_iota(jnp.int32, (ATT_BLOCK, LANES), 1)
        first = lane < ATT_HEAD_DIM

        def block_body(blk, carry):
            first_key = jnp.where(n * blocks + blk > 0, 0, ATT_BLOCK)
            valid = jnp.logical_and(band, kj >= first_key)
            lse_tile = jnp.zeros((ATT_BLOCK, LANES), F32)
            for hp in range(ATT_HEADS // 2):
                sl = slice(hp * LANES, (hp + 1) * LANES)
                q2 = load(q_ref, blk, sl) * (ATT_HEAD_DIM ** -0.5)
                k_cur = load(k_ref, blk, sl)
                v_cur = load(v_ref, blk, sl)
                kk = jnp.concatenate([kp_ref[:, sl], k_cur], axis=0)
                vv = jnp.concatenate([vp_ref[:, sl], v_cur], axis=0)
                outs = []
                for e in range(2):
                    qh = jnp.where(first if e == 0 else jnp.logical_not(first), q2, jnp.zeros_like(q2))
                    sc = lax.dot_general(qh, kk, (((1,), (1,)), ((), ())), preferred_element_type=F32)
                    sc = jnp.where(valid, sc, NEG_INF)
                    mx = jnp.max(sc, axis=-1, keepdims=True)
                    p = jnp.exp(sc - mx)
                    den = jnp.sum(p, axis=-1, keepdims=True)
                    o = jnp.dot(p.astype(BF16), vv, preferred_element_type=F32)
                    outs.append(o / den)
                    lse = jnp.broadcast_to(mx + jnp.log(den), (ATT_BLOCK, LANES))
                    lse_tile = jnp.where(lane == 2 * hp + e, lse, lse_tile)
                store(o_ref, blk, sl, jnp.where(first, outs[0], outs[1]).astype(BF16))
                kp_ref[:, sl] = k_cur
                vp_ref[:, sl] = v_cur
            store(lse_ref, blk, slice(0, LANES), lse_tile)
            return carry

        lax.fori_loop(0, blocks, block_body, 0)

    return attn_kernel


def _attention(qkv2, bsz, seq, g):
    d = ATT_DILATIONS[g]
    rows = WIN[d] // d
    lead = (bsz, seq // WIN[d], d, rows)
    pieces = Q_STEP // rows
    view = qkv2.reshape(lead + (N_QKV,))
    blk = (None, pieces, None, rows, ATT_W)

    def in_spec(kind):
        return pl.BlockSpec(blk, lambda i, r, n: (i, n, r, 0, 3 * g + kind))

    out_map = lambda i, r, n: (i, n, r, 0, 0)
    return pl.pallas_call(
        _make_attn_kernel(rows),
        grid=(bsz, d, seq // d // Q_STEP),
        in_specs=[in_spec(kind) for kind in range(3)],
        out_specs=[pl.BlockSpec(blk, out_map), pl.BlockSpec(blk[:-1] + (LANES,), out_map)],
        out_shape=[
            jax.ShapeDtypeStruct(lead + (ATT_W,), BF16),
            jax.ShapeDtypeStruct(lead + (LANES,), F32),
        ],
        scratch_shapes=[
            pltpu.VMEM((ATT_BLOCK, ATT_W), BF16),
            pltpu.VMEM((ATT_BLOCK, ATT_W), BF16),
        ],
        compiler_params=pltpu.CompilerParams(
            dimension_semantics=("parallel", "parallel", "arbitrary"), vmem_limit_bytes=VMEM_LIMIT),
        name=f"attn_d{d}",
    )(view, view, view)


def _merge_kernel(x_ref, yssd_ref, o1_ref, o4_ref, o16_ref, l1_ref, l4_ref, l16_ref,
                  zatt_ref, qmem_ref, zmem_ref, g0_ref, g1_ref, g2_ref, kv_ref,
                  wso_ref, wao_ref, wmo_ref, wout_ref, npost_ref, p4_ref, p16_ref, ex_ref,
                  out_ref, tok_ref):
    ssd = jnp.dot(yssd_ref[...], wso_ref[...], preferred_element_type=F32)
    merged = _sigmoid(g0_ref[...].astype(F32)) * ssd

    n4, n16 = TM_MERGE // 4, TM_MERGE // 16
    for r in range(4):
        tok_ref[0, pl.ds(r, n4, stride=4), :] = l4_ref[r]
    for r in range(16):
        tok_ref[1, pl.ds(r, n16, stride=16), :] = l16_ref[r]
    la, lb, lc = l1_ref[...], tok_ref[0], tok_ref[1]
    top = jnp.maximum(la, jnp.maximum(lb, lc))
    ea, eb, ec = jnp.exp(la - top), jnp.exp(lb - top), jnp.exp(lc - top)
    inv = 1.0 / (ea + eb + ec)

    def per_channel(wt):
        hi = wt.astype(BF16)
        lo = (wt - hi.astype(F32)).astype(BF16)
        return jnp.dot(jnp.concatenate([hi, lo], axis=1), ex_ref[...], preferred_element_type=F32)

    o4 = jnp.dot(p4_ref[...], o4_ref[...].reshape(TM_MERGE, ATT_W), preferred_element_type=F32)
    o16 = jnp.dot(p16_ref[...], o16_ref[...].reshape(TM_MERGE, ATT_W), preferred_element_type=F32)
    o = (per_channel(ea * inv) * o1_ref[...].astype(F32) + per_channel(eb * inv) * o4
         + per_channel(ec * inv) * o16)
    y_att = (o * _silu(zatt_ref[...].astype(F32))).astype(BF16)
    att = jnp.dot(y_att, wao_ref[...], preferred_element_type=F32)
    merged = merged + _sigmoid(g1_ref[...].astype(F32)) * att

    heads = []
    for h in range(MEM_HEADS):
        sl = slice(h * MEM_HEAD_DIM, (h + 1) * MEM_HEAD_DIM)
        q = qmem_ref[:, sl] * (MEM_HEAD_DIM ** -0.5)
        k = kv_ref[0, :, sl]
        vv = kv_ref[0, :, MEM_W + h * MEM_HEAD_DIM:MEM_W + (h + 1) * MEM_HEAD_DIM]
        sc = lax.dot_general(q, k, (((1,), (1,)), ((), ())), preferred_element_type=F32)
        mx = jnp.max(sc, axis=-1, keepdims=True)
        p = jnp.exp(sc - mx)
        den = jnp.sum(p, axis=-1, keepdims=True)
        heads.append(jnp.dot(p.astype(BF16), vv, preferred_element_type=F32) / den)
    o_mem = jnp.concatenate(heads, axis=1)
    y_mem = (o_mem * _silu(zmem_ref[...].astype(F32))).astype(BF16)
    mem = jnp.dot(y_mem, wmo_ref[...], preferred_element_type=F32)
    merged = merged + _sigmoid(g2_ref[...].astype(F32)) * mem

    out = jnp.dot(merged.astype(BF16), wout_ref[...], preferred_element_type=F32)
    ms = jnp.mean(out * out, axis=-1, keepdims=True)
    out_ref[...] = x_ref[...] + out * lax.rsqrt(ms + EPS) * npost_ref[...]


def _stream_to_token_matrix(d, tm):
    tok = jnp.arange(tm)
    src = (tok % d) * (tm // d) + tok // d
    return (src[:, None] == jnp.arange(tm)[None, :]).astype(BF16)


def _head_to_channel_matrix():
    lane = jnp.arange(2 * LANES) % LANES
    chan_head = jnp.arange(ATT_W) // ATT_HEAD_DIM
    return (lane[:, None] == chan_head[None, :]).astype(BF16)


def _merge(x2, y_ssd, o1, l1, o4, l4, o16, l16, main2, kv3, w_so, w_ao, w_mo, w_out, n_post, seq):
    t = x2.shape[0]
    tm = TM_MERGE
    per_batch = seq // tm
    row = lambda i: (i, 0)
    const = lambda i: (0, 0)

    def mblk(j):
        return pl.BlockSpec((tm, CB), lambda i: (i, j))

    def stream_spec(d, width):
        per_win = WIN[d] // tm
        return pl.BlockSpec((None, None, d, tm // d, width),
                            lambda i: (i // per_batch, (i % per_batch) // per_win, 0, i % per_win, 0))

    return pl.pallas_call(
        _merge_kernel,
        grid=(t // tm,),
        in_specs=[
            pl.BlockSpec((tm, D_MODEL), row),
            pl.BlockSpec((tm, D_INNER), row),
            pl.BlockSpec((tm, ATT_W), row), stream_spec(4, ATT_W), stream_spec(16, ATT_W),
            pl.BlockSpec((tm, LANES), row), stream_spec(4, LANES), stream_spec(16, LANES),
            mblk(BLK_ZATT), mblk(BLK_QMEM), mblk(BLK_ZMEM),
            mblk(BLK_GATE), mblk(BLK_GATE + 1), mblk(BLK_GATE + 2),
            pl.BlockSpec((1, MEM_LEN, 2 * MEM_W), lambda i: (i // per_batch, 0, 0)),
            pl.BlockSpec((D_INNER, D_MODEL), const),
            pl.BlockSpec((ATT_W, D_MODEL), const),
            pl.BlockSpec((MEM_W, D_MODEL), const),
            pl.BlockSpec((D_MODEL, D_MODEL), const),
            pl.BlockSpec((1, D_MODEL), const),
            pl.BlockSpec((tm, tm), const),
            pl.BlockSpec((tm, tm), const),
            pl.BlockSpec((2 * LANES, ATT_W), const),
        ],
        out_specs=pl.BlockSpec((tm, D_MODEL), row),
        out_shape=jax.ShapeDtypeStruct((t, D_MODEL), F32),
        scratch_shapes=[pltpu.VMEM((2, tm, LANES), F32)],
        compiler_params=pltpu.CompilerParams(
            dimension_semantics=("parallel",), vmem_limit_bytes=VMEM_LIMIT),
        name="merge",
    )(x2, y_ssd, o1, o4, o16, l1, l4, l16, main2, main2, main2, main2, main2, main2, kv3,
      w_so, w_ao, w_mo, w_out, n_post, _stream_to_token_matrix(4, tm), _stream_to_token_matrix(16, tm),
      _head_to_channel_matrix())


def _pad_lanes(v, width):
    return jnp.pad(v, ((0, 0), (0, width - v.shape[-1])))


def _regroup_w_in(w):
    cols = [w[:, :OFF_DT], w[:, OFF_ZATT:]]
    for g in range(ATT_GROUPS):
        for kind in range(3):
            start = OFF_QKV + (kind * ATT_GROUPS + g) * ATT_W
            cols.append(w[:, start:start + ATT_W])
    return jnp.concatenate(cols, axis=1).astype(BF16)


def kernel(x, mem, norm_pre, norm_post, w_in, conv_w, conv_b, dt_bias, a_log, d_skip, ssd_norm,
           w_ssd_out, w_attn_out, mem_norm, w_mem_kv, w_mem_out, w_out):
    bsz, seq, _ = x.shape
    depth = w_in.shape[0]
    t = bsz * seq
    assert seq % (16 * Q_STEP) == 0 and seq % TM_IN == 0 and WIN[4] % TM_MERGE == 0

    x2 = x.reshape(t, D_MODEL)
    mem2 = mem.reshape(bsz * MEM_LEN, D_MODEL)
    for l in range(depth):
        w = w_in[l]
        w_dt = _pad_lanes(w[:, OFF_DT:OFF_QKV], DT_PAD).astype(BF16)
        main2, qkv2, dt2 = _inproj(x2, norm_pre[l][None, :], _regroup_w_in(w), w_dt)
        main3 = main2.reshape(bsz, seq, N_MAIN)
        kv = _memkv(mem2, mem_norm[l][None, :], w_mem_kv[l].astype(BF16))

        y_ssd = _ssd(main3, dt2.reshape(bsz, seq, DT_PAD), conv_w[l], conv_b[l][None, :],
                     _pad_lanes(dt_bias[l][None, :], DT_PAD), _pad_lanes(a_log[l][None, :], DT_PAD),
                     jnp.repeat(d_skip[l], SSD_HEAD_DIM)[None, :], ssd_norm[l][None, :])

        o1, l1 = _attention(qkv2, bsz, seq, 0)
        o4, l4 = _attention(qkv2, bsz, seq, 1)
        o16, l16 = _attention(qkv2, bsz, seq, 2)

        x2 = _merge(x2, y_ssd.reshape(t, D_INNER), o1.reshape(t, ATT_W), l1.reshape(t, LANES),
                    o4, l4, o16, l16, main2, kv.reshape(bsz, MEM_LEN, 2 * MEM_W),
                    w_ssd_out[l].astype(BF16), w_attn_out[l].astype(BF16),
                    w_mem_out[l].astype(BF16), w_out[l].astype(BF16), norm_post[l][None, :], seq)
    return x2.reshape(bsz, seq, D_MODEL)
```

```python
import math

import jax
import jax.numpy as jnp
from jax import lax
from jax.experimental import pallas as pl
from jax.experimental.pallas import tpu as pltpu

F32 = jnp.float32
BF16 = jnp.bfloat16

EPS = 1e-6
D_MODEL = 1024

D_INNER = 2048
SSD_HEAD_DIM = 64
SSD_HEADS = 32
SSD_GROUPS = 8
D_STATE = 128
CONV_K = 4
CONV_DIM = D_INNER + 2 * SSD_GROUPS * D_STATE
CHUNK = 128

ATT_DILATIONS = (1, 4, 16)
ATT_GROUPS = 3
ATT_HEADS = 16
ATT_HEAD_DIM = 64
ATT_W = 1024
ATT_BLOCK = 128

MEM_LEN = 256
MEM_HEADS = 4
MEM_HEAD_DIM = 256
MEM_W = 1024

OFF_XBC = D_INNER
OFF_DT = OFF_XBC + CONV_DIM
OFF_QKV = OFF_DT + SSD_HEADS
OFF_ZATT = OFF_QKV + 3 * ATT_GROUPS * ATT_W

CB = 1024
BLK_ZATT = 6
BLK_QMEM = 7
BLK_ZMEM = 8
BLK_GATE = 9
N_MAIN = 12 * CB
N_QKV = 3 * ATT_GROUPS * ATT_W
CB_IN = 1536
MAIN_STEPS = N_MAIN // CB_IN
GROUP_STEPS = 3 * ATT_W // CB_IN
N_STEPS = MAIN_STEPS + ATT_GROUPS * GROUP_STEPS

LANES = 128
N_SLAB = D_MODEL // LANES
DT_PAD = LANES

TM_IN = 1024
TM_MERGE = 512
HALF_MERGE = TM_MERGE // 2
WIN = {1: ATT_BLOCK, 4: 4 * ATT_BLOCK, 16: TM_IN}
Q_STEP = 4 * ATT_BLOCK

VMEM_LIMIT = 56 * 1024 * 1024

NEG_INF = float("-inf")
LOG2E = math.log2(math.e)
Q_SCALE = LOG2E * ATT_HEAD_DIM ** -0.5


def _sigmoid(v):
    return 0.5 * jnp.tanh(0.5 * v) + 0.5


def _silu(v):
    return v * _sigmoid(v)


def _split_dot(lhs_bf16, v):
    hi = v.astype(BF16)
    r1 = v - hi.astype(F32)
    mid = r1.astype(BF16)
    lo = (r1 - mid.astype(F32)).astype(BF16)
    acc = jnp.dot(lhs_bf16, hi, preferred_element_type=F32)
    acc = acc + jnp.dot(lhs_bf16, mid, preferred_element_type=F32)
    return acc + jnp.dot(lhs_bf16, lo, preferred_element_type=F32)


def _inproj_kernel(x_ref, g_ref, w_ref, wdt_ref, main_ref, qkv_ref, dt_ref, h_ref, hs_ref):
    j = pl.program_id(1)

    @pl.when(j == 0)
    def _():
        x = x_ref[...]
        ms = jnp.mean(x * x, axis=-1, keepdims=True)
        h = x * lax.rsqrt(ms + EPS) * g_ref[...]
        hb = h.astype(BF16)
        h_ref[0] = hb
        dt_ref[...] = jnp.dot(hb, wdt_ref[...], preferred_element_type=F32)
        rows16 = TM_IN // 16
        for s in range(N_SLAB):
            sl = slice(s * LANES, (s + 1) * LANES)
            hs_ref[s] = h[:, sl]
            for w in range(TM_IN // WIN[4]):
                for r in range(4):
                    dst = w * WIN[4] + r * ATT_BLOCK
                    h_ref[1, dst:dst + ATT_BLOCK, sl] = (
                        hs_ref[s, pl.ds(w * WIN[4] + r, ATT_BLOCK, stride=4), :].astype(BF16))
            for r in range(16):
                h_ref[2, r * rows16:(r + 1) * rows16, sl] = (
                    hs_ref[s, pl.ds(r, rows16, stride=16), :].astype(BF16))

    @pl.when(j < MAIN_STEPS)
    def _():
        main_ref[...] = jnp.dot(h_ref[0], w_ref[...], preferred_element_type=F32).astype(BF16)

    @pl.when(j >= MAIN_STEPS)
    def _():
        group = (j - MAIN_STEPS) // GROUP_STEPS
        qkv_ref[...] = jnp.dot(h_ref[group], w_ref[...], preferred_element_type=F32).astype(BF16)


def _inproj(x2, gain, w_all, w_dt):
    t = x2.shape[0]
    return pl.pallas_call(
        _inproj_kernel,
        grid=(t // TM_IN, N_STEPS),
        in_specs=[
            pl.BlockSpec((TM_IN, D_MODEL), lambda i, j: (i, 0)),
            pl.BlockSpec((1, D_MODEL), lambda i, j: (0, 0)),
            pl.BlockSpec((D_MODEL, CB_IN), lambda i, j: (0, j)),
            pl.BlockSpec((D_MODEL, DT_PAD), lambda i, j: (0, 0)),
        ],
        out_specs=[
            pl.BlockSpec((TM_IN, CB_IN), lambda i, j: (i, jnp.minimum(j, MAIN_STEPS - 1))),
            pl.BlockSpec((TM_IN, CB_IN), lambda i, j: (i, jnp.maximum(j - MAIN_STEPS, 0))),
            pl.BlockSpec((TM_IN, DT_PAD), lambda i, j: (i, 0)),
        ],
        out_shape=[
            jax.ShapeDtypeStruct((t, N_MAIN), BF16),
            jax.ShapeDtypeStruct((t, N_QKV), BF16),
            jax.ShapeDtypeStruct((t, DT_PAD), F32),
        ],
        scratch_shapes=[
            pltpu.VMEM((3, TM_IN, D_MODEL), BF16),
            pltpu.VMEM((N_SLAB, TM_IN, LANES), F32),
        ],
        compiler_params=pltpu.CompilerParams(
            dimension_semantics=("parallel", "arbitrary"), vmem_limit_bytes=VMEM_LIMIT),
        name="inproj",
    )(x2, gain, w_all, w_dt)


def _memkv_kernel(m_ref, g_ref, w_ref, out_ref):
    m = m_ref[...]
    ms = jnp.mean(m * m, axis=-1, keepdims=True)
    h = (m * lax.rsqrt(ms + EPS) * g_ref[...]).astype(BF16)
    out_ref[...] = jnp.dot(h, w_ref[...], preferred_element_type=F32).astype(BF16)


def _memkv(mem2, gain, w_kv):
    rows = mem2.shape[0]
    return pl.pallas_call(
        _memkv_kernel,
        grid=(2 * MEM_W // CB,),
        in_specs=[
            pl.BlockSpec((rows, D_MODEL), lambda j: (0, 0)),
            pl.BlockSpec((1, D_MODEL), lambda j: (0, 0)),
            pl.BlockSpec((D_MODEL, CB), lambda j: (0, j)),
        ],
        out_specs=pl.BlockSpec((rows, CB), lambda j: (0, j)),
        out_shape=jax.ShapeDtypeStruct((rows, 2 * MEM_W), BF16),
        compiler_params=pltpu.CompilerParams(
            dimension_semantics=("arbitrary",), vmem_limit_bytes=VMEM_LIMIT),
        name="memkv",
    )(mem2, gain, w_kv)


N_PAIR = SSD_HEADS // 2
TAIL = 8
CONV_SLABS = CONV_DIM // LANES


def _ssd_kernel(z_ref, x_ref, bc_ref, dt_ref, cw_ref, cb_ref, dtb_ref, alog_ref, dskip_ref,
                nrm_ref, y_ref, state_ref, cbuf_ref, act_ref):
    @pl.when(pl.program_id(1) == 0)
    def _():
        state_ref[...] = jnp.zeros_like(state_ref)
        cbuf_ref[:, 0:TAIL, :] = jnp.zeros((CONV_SLABS, TAIL, LANES), F32)

    for s in range(CONV_SLABS):
        sl = slice(s * LANES, (s + 1) * LANES)
        src = x_ref if s < D_INNER // LANES else bc_ref
        off = (s * LANES) % D_INNER
        cbuf_ref[s, TAIL:TAIL + CHUNK, :] = src[0, :, off:off + LANES].astype(F32)
        conv = cb_ref[:, sl]
        for k in range(CONV_K):
            lo = TAIL - (CONV_K - 1) + k
            conv = conv + cw_ref[k:k + 1, sl] * cbuf_ref[s, pl.ds(lo, CHUNK, stride=1), :]
        cbuf_ref[s, 0:TAIL, :] = cbuf_ref[s, CHUNK:CHUNK + TAIL, :]
        act_ref[:, sl] = _silu(conv)

    v = dt_ref[0] + dtb_ref[...]
    dt = jnp.maximum(v, 0.0) + jnp.log1p(jnp.exp(-jnp.abs(v)))
    a2 = -jnp.exp(alog_ref[...]) * LOG2E
    row = lax.broadcasted_iota(jnp.int32, (CHUNK, CHUNK), 0)
    col = lax.broadcasted_iota(jnp.int32, (CHUNK, CHUNK), 1)
    causal = row >= col
    tri = jnp.where(causal, 1.0, 0.0).astype(BF16)
    acum = _split_dot(tri, dt * a2)
    total = acum[CHUNK - 1:CHUNK, :]
    acum_t = acum.T
    dt_t = dt.T
    dsts_t = (jnp.exp2(total - acum) * dt).T
    cdec = jnp.exp2(total)

    lane = lax.broadcasted_iota(jnp.int32, (CHUNK, LANES), 1)
    first = lane < SSD_HEAD_DIM

    for g in range(SSD_GROUPS):
        b_f = act_ref[:, D_INNER + g * D_STATE:D_INNER + (g + 1) * D_STATE]
        c_f = act_ref[:, D_INNER + (SSD_GROUPS + g) * D_STATE:D_INNER + (SSD_GROUPS + g + 1) * D_STATE]
        b_t = b_f.T
        cbm = lax.dot_general(c_f.astype(BF16), b_f.astype(BF16), (((1,), (1,)), ((), ())),
                              preferred_element_type=F32)
        y_pairs = []
        for pp in range(2):
            pair = 2 * g + pp
            x_pair = act_ref[:, pair * LANES:(pair + 1) * LANES]
            s_prev = state_ref[pair]
            rhs = jnp.concatenate([x_pair, s_prev], axis=0).astype(BF16)
            x_b = x_pair.astype(BF16)
            ys, sts, decs = [], [], []
            for e in range(2):
                h = 2 * pair + e
                acb = jnp.broadcast_to(acum[:, h:h + 1], (CHUNK, CHUNK))
                seg = acb - acum_t[h:h + 1, :]
                decay = jnp.exp2(jnp.where(causal, seg, NEG_INF))
                m = cbm * decay * dt_t[h:h + 1, :]
                c_exp = c_f * jnp.exp2(acb)
                lhs = jnp.concatenate([m, c_exp], axis=1).astype(BF16)
                ys.append(jnp.dot(lhs, rhs, preferred_element_type=F32))
                bts = (b_t * dsts_t[h:h + 1, :]).astype(BF16)
                sts.append(jnp.dot(bts, x_b, preferred_element_type=F32))
                decs.append(jnp.broadcast_to(cdec[:, h:h + 1], (CHUNK, LANES)))
            state_ref[pair] = (s_prev * jnp.where(first, decs[0], decs[1])
                               + jnp.where(first, sts[0], sts[1]))
            y_pair = jnp.where(first, ys[0], ys[1])
            y_pairs.append(y_pair + dskip_ref[:, pair * LANES:(pair + 1) * LANES] * x_pair)

        gsl = slice(2 * g * LANES, (2 * g + 2) * LANES)
        yg = jnp.concatenate(y_pairs, axis=1) * _silu(z_ref[0, :, gsl].astype(F32))
        ms = jnp.mean(yg * yg, axis=-1, keepdims=True)
        y_ref[0, :, gsl] = (yg * lax.rsqrt(ms + EPS) * nrm_ref[:, gsl]).astype(BF16)


def _ssd(main3, dt3, conv_w, conv_b, dt_bias, a_log, d_skip_x, ssd_norm):
    b, s, _ = main3.shape
    nc = s // CHUNK
    wide = D_INNER
    return pl.pallas_call(
        _ssd_kernel,
        grid=(b, nc),
        in_specs=[
            pl.BlockSpec((1, CHUNK, wide), lambda i, c: (i, c, 0)),
            pl.BlockSpec((1, CHUNK, wide), lambda i, c: (i, c, 1)),
            pl.BlockSpec((1, CHUNK, wide), lambda i, c: (i, c, 2)),
            pl.BlockSpec((1, CHUNK, DT_PAD), lambda i, c: (i, c, 0)),
            pl.BlockSpec((CONV_K, CONV_DIM), lambda i, c: (0, 0)),
            pl.BlockSpec((1, CONV_DIM), lambda i, c: (0, 0)),
            pl.BlockSpec((1, DT_PAD), lambda i, c: (0, 0)),
            pl.BlockSpec((1, DT_PAD), lambda i, c: (0, 0)),
            pl.BlockSpec((1, D_INNER), lambda i, c: (0, 0)),
            pl.BlockSpec((1, D_INNER), lambda i, c: (0, 0)),
        ],
        out_specs=pl.BlockSpec((1, CHUNK, D_INNER), lambda i, c: (i, c, 0)),
        out_shape=jax.ShapeDtypeStruct((b, s, D_INNER), BF16),
        scratch_shapes=[
            pltpu.VMEM((N_PAIR, D_STATE, LANES), F32),
            pltpu.VMEM((CONV_SLABS, TAIL + CHUNK, LANES), F32),
            pltpu.VMEM((CHUNK, CONV_DIM), F32),
        ],
        compiler_params=pltpu.CompilerParams(
            dimension_semantics=("parallel", "arbitrary"), vmem_limit_bytes=VMEM_LIMIT),
        name="ssd",
    )(main3, main3, main3, dt3, conv_w, conv_b, dt_bias, a_log, d_skip_x, ssd_norm)


def _make_attn_kernel(rows):
    parts = ATT_BLOCK // rows
    blocks = Q_STEP // ATT_BLOCK

    def load(ref, blk, sl):
        pieces = [ref[blk * parts + p, :, sl] for p in range(parts)]
        return pieces[0] if parts == 1 else jnp.concatenate(pieces, axis=0)

    def store(ref, blk, sl, val):
        for p in range(parts):
            ref[blk * parts + p, :, sl] = val[p * rows:(p + 1) * rows]

    def attn_kernel(q_ref, k_ref, v_ref, o_ref, lse_ref, kp_ref, vp_ref):
        n = pl.program_id(2)

        @pl.when(n == 0)
        def _():
            kp_ref[...] = jnp.zeros_like(kp_ref)
            vp_ref[...] = jnp.zeros_like(vp_ref)

        qi = lax.broadcasted_iota(jnp.int32, (ATT_BLOCK, 2 * ATT_BLOCK), 0)
        kj = lax.broadcasted_iota(jnp.int32, (ATT_BLOCK, 2 * ATT_BLOCK), 1)
        band = jnp.logical_and(kj >= qi, kj - ATT_BLOCK <= qi)
        lane = lax.broadcasted_iota(jnp.int32, (ATT_BLOCK, LANES), 1)
        first = lane < ATT_HEAD_DIM
        first_kv = lax.broadcasted_iota(jnp.int32, (2 * ATT_BLOCK, LANES), 1) < ATT_HEAD_DIM

        def block_body(blk, carry):
            first_key = jnp.where(n * blocks + blk > 0, 0, ATT_BLOCK)
            valid = jnp.logical_and(band, kj >= first_key)
            pairs, scores, probs, results, maxes = {}, {}, {}, {}, {}

            def load_pair(hp):
                sl = slice(hp * LANES, (hp + 1) * LANES)
                k_cur = load(k_ref, blk, sl)
                v_cur = load(v_ref, blk, sl)
                kk = jnp.concatenate([kp_ref[:, sl], k_cur], axis=0)
                vv = jnp.concatenate([vp_ref[:, sl], v_cur], axis=0)
                kp_ref[:, sl] = k_cur
                vp_ref[:, sl] = v_cur
                ones = jnp.ones_like(vv)
                v_ones = (jnp.where(first_kv, vv, ones), jnp.where(first_kv, ones, vv))
                pairs[hp] = (load(q_ref, blk, sl), kk, v_ones)

            def score(h):
                hp, e = divmod(h, 2)
                if e == 0:
                    load_pair(hp)
                q2, kk, _ = pairs[hp]
                qh = jnp.where(first if e == 0 else jnp.logical_not(first), q2, jnp.zeros_like(q2))
                scores[h] = lax.dot_general(qh, kk, (((1,), (1,)), ((), ())), preferred_element_type=F32)

            def softmax_numerator(h):
                sc = jnp.where(valid, scores.pop(h), NEG_INF)
                maxes[h] = jnp.max(sc, axis=-1, keepdims=True)
                probs[h] = jnp.exp2(sc - maxes[h]).astype(BF16)

            def weighted_values(h):
                hp, e = divmod(h, 2)
                results[h] = jnp.dot(probs.pop(h), pairs[hp][2][e], preferred_element_type=F32)

            def finish_pair(hp, lse_tile):
                r0, r1 = results.pop(2 * hp), results.pop(2 * hp + 1)
                den_other = jnp.where(first, r1, r0)
                den = pltpu.roll(den_other, ATT_HEAD_DIM, 1)
                o = jnp.where(first, r0, r1) / den
                store(o_ref, blk, slice(hp * LANES, (hp + 1) * LANES), o.astype(BF16))
                lse_tile = jnp.where(lane == 2 * hp, maxes.pop(2 * hp) + jnp.log2(den), lse_tile)
                lse_tile = jnp.where(lane == 2 * hp + 1, maxes.pop(2 * hp + 1) + jnp.log2(den_other), lse_tile)
                del pairs[hp]
                return lse_tile

            lse_tile = jnp.zeros((ATT_BLOCK, LANES), F32)
            for t in range(ATT_HEADS + 3):
                if t < ATT_HEADS:
                    score(t)
                if 0 <= t - 1 < ATT_HEADS:
                    softmax_numerator(t - 1)
                if 0 <= t - 2 < ATT_HEADS:
                    weighted_values(t - 2)
                if 0 <= t - 3 < ATT_HEADS and (t - 3) % 2 == 1:
                    lse_tile = finish_pair((t - 3) // 2, lse_tile)
            store(lse_ref, blk, slice(0, LANES), lse_tile)
            return carry

        lax.fori_loop(0, blocks, block_body, 0)

    return attn_kernel


def _attention(qkv2, bsz, seq, g):
    d = ATT_DILATIONS[g]
    rows = WIN[d] // d
    lead = (bsz, seq // WIN[d], d, rows)
    pieces = Q_STEP // rows
    view = qkv2.reshape(lead + (N_QKV,))
    blk = (None, pieces, None, rows, ATT_W)

    def in_spec(kind):
        return pl.BlockSpec(blk, lambda i, r, n: (i, n, r, 0, 3 * g + kind))

    out_map = lambda i, r, n: (i, n, r, 0, 0)
    return pl.pallas_call(
        _make_attn_kernel(rows),
        grid=(bsz, d, seq // d // Q_STEP),
        in_specs=[in_spec(kind) for kind in range(3)],
        out_specs=[pl.BlockSpec(blk, out_map), pl.BlockSpec(blk[:-1] + (LANES,), out_map)],
        out_shape=[
            jax.ShapeDtypeStruct(lead + (ATT_W,), BF16),
            jax.ShapeDtypeStruct(lead + (LANES,), F32),
        ],
        scratch_shapes=[
            pltpu.VMEM((ATT_BLOCK, ATT_W), BF16),
            pltpu.VMEM((ATT_BLOCK, ATT_W), BF16),
        ],
        compiler_params=pltpu.CompilerParams(
            dimension_semantics=("parallel", "parallel", "arbitrary"), vmem_limit_bytes=VMEM_LIMIT),
        name=f"attn_d{d}",
    )(view, view, view)


def _merge_kernel(x_ref, yssd_ref, o1_ref, o4_ref, o16_ref, l1_ref, l4_ref, l16_ref,
                  zatt_ref, qmem_ref, zmem_ref, g0_ref, g1_ref, g2_ref, kv_ref,
                  wso_ref, wao_ref, wmo_ref, wout_ref, npost_ref, p4_ref, p16_ref, ex_ref,
                  out_ref, tok_ref):
    n4, n16 = HALF_MERGE // 4, HALF_MERGE // 16

    def per_channel(wt):
        hi = wt.astype(BF16)
        lo = (wt - hi.astype(F32)).astype(BF16)
        return jnp.dot(jnp.concatenate([hi, lo], axis=1), ex_ref[...], preferred_element_type=F32)

    def steps(hf):
        rs = slice(hf * HALF_MERGE, (hf + 1) * HALF_MERGE)
        v = {}

        def weights():
            for r in range(4):
                tok_ref[2 * hf, pl.ds(r, n4, stride=4), :] = l4_ref[r, hf * n4:(hf + 1) * n4, :]
            for r in range(16):
                tok_ref[2 * hf + 1, pl.ds(r, n16, stride=16), :] = l16_ref[r, hf * n16:(hf + 1) * n16, :]
            la, lb, lc = l1_ref[rs, :], tok_ref[2 * hf], tok_ref[2 * hf + 1]
            top = jnp.maximum(la, jnp.maximum(lb, lc))
            ea, eb, ec = jnp.exp2(la - top), jnp.exp2(lb - top), jnp.exp2(lc - top)
            inv = 1.0 / (ea + eb + ec)
            v["w4"], v["w16"] = per_channel(eb * inv), per_channel(ec * inv)
            o4 = o4_ref[:, hf * n4:(hf + 1) * n4, :].reshape(HALF_MERGE, ATT_W)
            o16 = o16_ref[:, hf * n16:(hf + 1) * n16, :].reshape(HALF_MERGE, ATT_W)
            v["o4"] = jnp.dot(p4_ref[...], o4, preferred_element_type=F32)
            v["o16"] = jnp.dot(p16_ref[...], o16, preferred_element_type=F32)

        def combine():
            w4, w16 = v.pop("w4"), v.pop("w16")
            o = ((1.0 - w4 - w16) * o1_ref[rs, :].astype(F32) + w4 * v.pop("o4")
                 + w16 * v.pop("o16"))
            v["y_att"] = (o * _silu(zatt_ref[rs, :].astype(F32))).astype(BF16)

        def ssd_att_proj():
            v["ssd"] = jnp.dot(yssd_ref[rs, :], wso_ref[...], preferred_element_type=F32)
            v["att"] = jnp.dot(v.pop("y_att"), wao_ref[...], preferred_element_type=F32)

        def mem_attention():
            heads = []
            for h in range(MEM_HEADS):
                sl = slice(h * MEM_HEAD_DIM, (h + 1) * MEM_HEAD_DIM)
                q = qmem_ref[rs, sl] * (MEM_HEAD_DIM ** -0.5)
                k = kv_ref[0, :, sl]
                vv = kv_ref[0, :, MEM_W + h * MEM_HEAD_DIM:MEM_W + (h + 1) * MEM_HEAD_DIM]
                sc = lax.dot_general(q, k, (((1,), (1,)), ((), ())), preferred_element_type=F32)
                mx = jnp.max(sc, axis=-1, keepdims=True)
                p = jnp.exp(sc - mx)
                den = jnp.sum(p, axis=-1, keepdims=True)
                heads.append(jnp.dot(p.astype(BF16), vv, preferred_element_type=F32) / den)
            o_mem = jnp.concatenate(heads, axis=1)
            v["y_mem"] = (o_mem * _silu(zmem_ref[rs, :].astype(F32))).astype(BF16)

        def mem_proj():
            v["mem"] = jnp.dot(v.pop("y_mem"), wmo_ref[...], preferred_element_type=F32)

        def gates():
            merged = (_sigmoid(g0_ref[rs, :].astype(F32)) * v.pop("ssd")
                      + _sigmoid(g1_ref[rs, :].astype(F32)) * v.pop("att")
                      + _sigmoid(g2_ref[rs, :].astype(F32)) * v.pop("mem"))
            v["merged"] = merged.astype(BF16)

        def out_proj():
            v["out"] = jnp.dot(v.pop("merged"), wout_ref[...], preferred_element_type=F32)

        def finish():
            out = v.pop("out")
            ms = jnp.mean(out * out, axis=-1, keepdims=True)
            out_ref[rs, :] = x_ref[rs, :] + out * lax.rsqrt(ms + EPS) * npost_ref[...]

        return [weights, combine, ssd_att_proj, mem_attention, mem_proj, gates, out_proj, finish]

    first_half, second_half = steps(0), steps(1)
    for t in range(len(first_half) + 1):
        if t < len(first_half):
            first_half[t]()
        if t >= 1:
            second_half[t - 1]()


def _stream_to_token_matrix(d, tm):
    tok = jnp.arange(tm)
    src = (tok % d) * (tm // d) + tok // d
    return (src[:, None] == jnp.arange(tm)[None, :]).astype(BF16)


def _head_to_channel_matrix():
    lane = jnp.arange(2 * LANES) % LANES
    chan_head = jnp.arange(ATT_W) // ATT_HEAD_DIM
    return (lane[:, None] == chan_head[None, :]).astype(BF16)


def _merge(x2, y_ssd, o1, l1, o4, l4, o16, l16, main2, kv3, w_so, w_ao, w_mo, w_out, n_post, seq):
    t = x2.shape[0]
    tm = TM_MERGE
    per_batch = seq // tm
    row = lambda i: (i, 0)
    const = lambda i: (0, 0)

    def mblk(j):
        return pl.BlockSpec((tm, CB), lambda i: (i, j))

    def stream_spec(d, width):
        per_win = WIN[d] // tm
        return pl.BlockSpec((None, None, d, tm // d, width),
                            lambda i: (i // per_batch, (i % per_batch) // per_win, 0, i % per_win, 0))

    return pl.pallas_call(
        _merge_kernel,
        grid=(t // tm,),
        in_specs=[
            pl.BlockSpec((tm, D_MODEL), row),
            pl.BlockSpec((tm, D_INNER), row),
            pl.BlockSpec((tm, ATT_W), row), stream_spec(4, ATT_W), stream_spec(16, ATT_W),
            pl.BlockSpec((tm, LANES), row), stream_spec(4, LANES), stream_spec(16, LANES),
            mblk(BLK_ZATT), mblk(BLK_QMEM), mblk(BLK_ZMEM),
            mblk(BLK_GATE), mblk(BLK_GATE + 1), mblk(BLK_GATE + 2),
            pl.BlockSpec((1, MEM_LEN, 2 * MEM_W), lambda i: (i // per_batch, 0, 0)),
            pl.BlockSpec((D_INNER, D_MODEL), const),
            pl.BlockSpec((ATT_W, D_MODEL), const),
            pl.BlockSpec((MEM_W, D_MODEL), const),
            pl.BlockSpec((D_MODEL, D_MODEL), const),
            pl.BlockSpec((1, D_MODEL), const),
            pl.BlockSpec((HALF_MERGE, HALF_MERGE), const),
            pl.BlockSpec((HALF_MERGE, HALF_MERGE), const),
            pl.BlockSpec((2 * LANES, ATT_W), const),
        ],
        out_specs=pl.BlockSpec((tm, D_MODEL), row),
        out_shape=jax.ShapeDtypeStruct((t, D_MODEL), F32),
        scratch_shapes=[pltpu.VMEM((4, HALF_MERGE, LANES), F32)],
        compiler_params=pltpu.CompilerParams(
            dimension_semantics=("parallel",), vmem_limit_bytes=VMEM_LIMIT),
        name="merge",
    )(x2, y_ssd, o1, o4, o16, l1, l4, l16, main2, main2, main2, main2, main2, main2, kv3,
      w_so, w_ao, w_mo, w_out, n_post, _stream_to_token_matrix(4, HALF_MERGE),
      _stream_to_token_matrix(16, HALF_MERGE),
      _head_to_channel_matrix())


def _pad_lanes(v, width):
    return jnp.pad(v, ((0, 0), (0, width - v.shape[-1])))


def _regroup_w_in(w):
    cols = [w[:, :OFF_DT], w[:, OFF_ZATT:]]
    for g in range(ATT_GROUPS):
        for kind in range(3):
            start = OFF_QKV + (kind * ATT_GROUPS + g) * ATT_W
            sec = w[:, start:start + ATT_W]
            cols.append(sec * Q_SCALE if kind == 0 else sec)
    return jnp.concatenate(cols, axis=1).astype(BF16)


def kernel(x, mem, norm_pre, norm_post, w_in, conv_w, conv_b, dt_bias, a_log, d_skip, ssd_norm,
           w_ssd_out, w_attn_out, mem_norm, w_mem_kv, w_mem_out, w_out):
    bsz, seq, _ = x.shape
    depth = w_in.shape[0]
    t = bsz * seq
    assert seq % (16 * Q_STEP) == 0 and seq % TM_IN == 0 and WIN[4] % TM_MERGE == 0

    x2 = x.reshape(t, D_MODEL)
    mem2 = mem.reshape(bsz * MEM_LEN, D_MODEL)
    for l in range(depth):
        w = w_in[l]
        w_dt = _pad_lanes(w[:, OFF_DT:OFF_QKV], DT_PAD).astype(BF16)
        main2, qkv2, dt2 = _inproj(x2, norm_pre[l][None, :], _regroup_w_in(w), w_dt)
        main3 = main2.reshape(bsz, seq, N_MAIN)
        kv = _memkv(mem2, mem_norm[l][None, :], w_mem_kv[l].astype(BF16))

        y_ssd = _ssd(main3, dt2.reshape(bsz, seq, DT_PAD), conv_w[l], conv_b[l][None, :],
                     _pad_lanes(dt_bias[l][None, :], DT_PAD), _pad_lanes(a_log[l][None, :], DT_PAD),
                     jnp.repeat(d_skip[l], SSD_HEAD_DIM)[None, :], ssd_norm[l][None, :])

        o1, l1 = _attention(qkv2, bsz, seq, 0)
        o4, l4 = _attention(qkv2, bsz, seq, 1)
        o16, l16 = _attention(qkv2, bsz, seq, 2)

        x2 = _merge(x2, y_ssd.reshape(t, D_INNER), o1.reshape(t, ATT_W), l1.reshape(t, LANES),
                    o4, l4, o16, l16, main2, kv.reshape(bsz, MEM_LEN, 2 * MEM_W),
                    w_ssd_out[l].astype(BF16), w_attn_out[l].astype(BF16),
                    w_mem_out[l].astype(BF16), w_out[l].astype(BF16), norm_post[l][None, :], seq)
    return x2.reshape(bsz, seq, D_MODEL)
```

```python
import math

import jax
import jax.numpy as jnp
from jax import lax
from jax.experimental import pallas as pl
from jax.experimental.pallas import tpu as pltpu

F32 = jnp.float32
BF16 = jnp.bfloat16

EPS = 1e-6
D_MODEL = 1024

D_INNER = 2048
SSD_HEAD_DIM = 64
SSD_HEADS = 32
SSD_GROUPS = 8
D_STATE = 128
CONV_K = 4
CONV_DIM = D_INNER + 2 * SSD_GROUPS * D_STATE
CHUNK = 128

ATT_DILATIONS = (1, 4, 16)
ATT_GROUPS = 3
ATT_HEADS = 16
ATT_HEAD_DIM = 64
ATT_W = 1024
ATT_BLOCK = 128

MEM_LEN = 256
MEM_HEADS = 4
MEM_HEAD_DIM = 256
MEM_W = 1024

OFF_XBC = D_INNER
OFF_DT = OFF_XBC + CONV_DIM
OFF_QKV = OFF_DT + SSD_HEADS
OFF_ZATT = OFF_QKV + 3 * ATT_GROUPS * ATT_W

CB = 1024
BLK_ZATT = 0
BLK_QMEM = 1
BLK_ZMEM = 2
BLK_GATE = 3
N_SSD_IN = D_INNER + CONV_DIM
N_MAIN = 6 * CB
N_QKV = 3 * ATT_GROUPS * ATT_W
CB_IN = 1536
SSD_STEPS = N_SSD_IN // CB_IN
MAIN_STEPS = N_MAIN // CB_IN
GROUP_STEPS = 3 * ATT_W // CB_IN
MAIN0 = SSD_STEPS
QKV0 = SSD_STEPS + MAIN_STEPS
N_STEPS = QKV0 + ATT_GROUPS * GROUP_STEPS

LANES = 128
N_SLAB = D_MODEL // LANES
DT_PAD = LANES

TM_IN = 1024
TM_MERGE = 512
HALF_MERGE = TM_MERGE // 2
WIN = {1: ATT_BLOCK, 4: 4 * ATT_BLOCK, 16: TM_IN}
Q_STEP = 4 * ATT_BLOCK

VMEM_LIMIT = 58 * 1024 * 1024

NEG_INF = float("-inf")
LOG2E = math.log2(math.e)
Q_SCALE = LOG2E * ATT_HEAD_DIM ** -0.5


def _sigmoid(v):
    return 0.5 * jnp.tanh(0.5 * v) + 0.5


def _silu(v):
    return v * _sigmoid(v)


def _split_dot(lhs_bf16, v):
    hi = v.astype(BF16)
    r1 = v - hi.astype(F32)
    mid = r1.astype(BF16)
    lo = (r1 - mid.astype(F32)).astype(BF16)
    acc = jnp.dot(lhs_bf16, hi, preferred_element_type=F32)
    acc = acc + jnp.dot(lhs_bf16, mid, preferred_element_type=F32)
    return acc + jnp.dot(lhs_bf16, lo, preferred_element_type=F32)


N_PAIR = SSD_HEADS // 2
TAIL = 8
CONV_SLABS = CONV_DIM // LANES
CHUNKS_PER_TILE = TM_IN // CHUNK
MM_PIECE = 256


def _ssd_chunk(slab, dt_rows, cw_ref, cb_ref, dtb_ref, alog_ref, dskip_ref, nrm_ref, y_ref,
               state_ref, tail_ref, cbuf_ref, act_ref, between_groups):
    for s in range(CONV_SLABS):
        sl = slice(s * LANES, (s + 1) * LANES)
        cbuf_ref[s, 0:TAIL, :] = tail_ref[s]
        cbuf_ref[s, TAIL:TAIL + CHUNK, :] = slab(D_INNER + s * LANES).astype(F32)
        conv = cb_ref[:, sl]
        for k in range(CONV_K):
            lo = TAIL - (CONV_K - 1) + k
            conv = conv + cw_ref[k:k + 1, sl] * cbuf_ref[s, pl.ds(lo, CHUNK, stride=1), :]
        tail_ref[s] = cbuf_ref[s, CHUNK:CHUNK + TAIL, :]
        act_ref[:, sl] = _silu(conv)

    v = dt_rows + dtb_ref[...]
    dt = jnp.maximum(v, 0.0) + jnp.log1p(jnp.exp(-jnp.abs(v)))
    a2 = -jnp.exp(alog_ref[...]) * LOG2E
    row = lax.broadcasted_iota(jnp.int32, (CHUNK, CHUNK), 0)
    col = lax.broadcasted_iota(jnp.int32, (CHUNK, CHUNK), 1)
    causal = row >= col
    tri = jnp.where(causal, 1.0, 0.0).astype(BF16)
    acum = _split_dot(tri, dt * a2)
    total = acum[CHUNK - 1:CHUNK, :]
    acum_t = acum.T
    dt_t = dt.T
    dsts_t = (jnp.exp2(total - acum) * dt).T
    cdec = jnp.exp2(total)

    lane = lax.broadcasted_iota(jnp.int32, (CHUNK, LANES), 1)
    first = lane < SSD_HEAD_DIM

    for g in range(SSD_GROUPS):
        between_groups(g)
        b_f = act_ref[:, D_INNER + g * D_STATE:D_INNER + (g + 1) * D_STATE]
        c_f = act_ref[:, D_INNER + (SSD_GROUPS + g) * D_STATE:D_INNER + (SSD_GROUPS + g + 1) * D_STATE]
        b_t = b_f.T
        cbm = lax.dot_general(c_f.astype(BF16), b_f.astype(BF16), (((1,), (1,)), ((), ())),
                              preferred_element_type=F32)
        y_pairs = []
        for pp in range(2):
            pair = 2 * g + pp
            x_pair = act_ref[:, pair * LANES:(pair + 1) * LANES]
            s_prev = state_ref[pair]
            rhs = jnp.concatenate([x_pair, s_prev], axis=0).astype(BF16)
            x_b = x_pair.astype(BF16)
            ys, sts, decs = [], [], []
            for e in range(2):
                h = 2 * pair + e
                acb = jnp.broadcast_to(acum[:, h:h + 1], (CHUNK, CHUNK))
                seg = acb - acum_t[h:h + 1, :]
                decay = jnp.exp2(jnp.where(causal, seg, NEG_INF))
                m = cbm * decay * dt_t[h:h + 1, :]
                c_exp = c_f * jnp.exp2(acb)
                lhs = jnp.concatenate([m, c_exp], axis=1).astype(BF16)
                ys.append(jnp.dot(lhs, rhs, preferred_element_type=F32))
                bts = (b_t * dsts_t[h:h + 1, :]).astype(BF16)
                sts.append(jnp.dot(bts, x_b, preferred_element_type=F32))
                decs.append(jnp.broadcast_to(cdec[:, h:h + 1], (CHUNK, LANES)))
            state_ref[pair] = (s_prev * jnp.where(first, decs[0], decs[1])
                               + jnp.where(first, sts[0], sts[1]))
            y_pair = jnp.where(first, ys[0], ys[1])
            y_pairs.append(y_pair + dskip_ref[:, pair * LANES:(pair + 1) * LANES] * x_pair)

        gsl = slice(2 * g * LANES, (2 * g + 2) * LANES)
        z = jnp.concatenate([slab(2 * g * LANES), slab((2 * g + 1) * LANES)], axis=1).astype(F32)
        yg = jnp.concatenate(y_pairs, axis=1) * _silu(z)
        ms = jnp.mean(yg * yg, axis=-1, keepdims=True)
        y_ref[:, gsl] = (yg * lax.rsqrt(ms + EPS) * nrm_ref[:, gsl]).astype(BF16)


def _make_inproj_ssd_kernel(tiles_per_batch):
    def kernel_body(x_ref, g_ref, w_ref, wdt_ref, cw_ref, cb_ref, dtb_ref, alog_ref, dskip_ref, nrm_ref,
                    main_ref, qkv_ref, y_ref,
                    h_ref, hs_ref, zx_ref, dt_ref, state_ref, tail_ref, cbuf_ref, act_ref):
        i = pl.program_id(0)
        j = pl.program_id(1)

        @pl.when(j == 0)
        def _():
            x = x_ref[...]
            ms = jnp.mean(x * x, axis=-1, keepdims=True)
            h = x * lax.rsqrt(ms + EPS) * g_ref[...]
            hb = h.astype(BF16)
            h_ref[0] = hb
            dt_ref[...] = jnp.dot(hb, wdt_ref[...], preferred_element_type=F32)
            rows16 = TM_IN // 16
            for s in range(N_SLAB):
                sl = slice(s * LANES, (s + 1) * LANES)
                hs_ref[s] = h[:, sl]
                for w in range(TM_IN // WIN[4]):
                    for r in range(4):
                        dst = w * WIN[4] + r * ATT_BLOCK
                        h_ref[1, dst:dst + ATT_BLOCK, sl] = (
                            hs_ref[s, pl.ds(w * WIN[4] + r, ATT_BLOCK, stride=4), :].astype(BF16))
                for r in range(16):
                    h_ref[2, r * rows16:(r + 1) * rows16, sl] = (
                        hs_ref[s, pl.ds(r, rows16, stride=16), :].astype(BF16))

            @pl.when(i % tiles_per_batch == 0)
            def _():
                state_ref[...] = jnp.zeros_like(state_ref)
                tail_ref[...] = jnp.zeros_like(tail_ref)

        def project(which, out_ref, cols):
            out_ref[:, cols] = jnp.dot(h_ref[which], w_ref[:, cols],
                                       preferred_element_type=F32).astype(BF16)

        def with_ssd_chunk(which, out_ref):
            r0 = pl.multiple_of((j - MAIN0) * CHUNK, CHUNK)

            def slab(c):
                return zx_ref[c // CB_IN, pl.ds(r0, CHUNK), c % CB_IN:c % CB_IN + LANES]

            def between_groups(g):
                if g < CB_IN // MM_PIECE:
                    project(which, out_ref, slice(g * MM_PIECE, (g + 1) * MM_PIECE))

            _ssd_chunk(slab, dt_ref[pl.ds(r0, CHUNK), :], cw_ref, cb_ref, dtb_ref, alog_ref, dskip_ref,
                       nrm_ref, y_ref, state_ref, tail_ref, cbuf_ref, act_ref, between_groups)

        group = jnp.maximum(j - QKV0, 0) // GROUP_STEPS

        @pl.when(j < MAIN0)
        def _():
            zx_ref[j] = jnp.dot(h_ref[0], w_ref[...], preferred_element_type=F32).astype(BF16)

        @pl.when(jnp.logical_and(j >= MAIN0, j < QKV0))
        def _():
            with_ssd_chunk(0, main_ref)

        @pl.when(jnp.logical_and(j >= QKV0, j < MAIN0 + CHUNKS_PER_TILE))
        def _():
            with_ssd_chunk(group, qkv_ref)

        @pl.when(j >= MAIN0 + CHUNKS_PER_TILE)
        def _():
            project(group, qkv_ref, slice(0, CB_IN))

    return kernel_body


def _inproj_ssd(x2, gain, w_all, w_dt, conv_w, conv_b, dt_bias, a_log, d_skip_x, ssd_norm, seq):
    t = x2.shape[0]
    const = lambda i, j: (0, 0)
    return pl.pallas_call(
        _make_inproj_ssd_kernel(seq // TM_IN),
        grid=(t // TM_IN, N_STEPS),
        in_specs=[
            pl.BlockSpec((TM_IN, D_MODEL), lambda i, j: (i, 0), pipeline_mode=pl.Buffered(1)),
            pl.BlockSpec((1, D_MODEL), const),
            pl.BlockSpec((D_MODEL, CB_IN), lambda i, j: (0, j)),
            pl.BlockSpec((D_MODEL, DT_PAD), const),
            pl.BlockSpec((CONV_K, CONV_DIM), const),
            pl.BlockSpec((1, CONV_DIM), const),
            pl.BlockSpec((1, DT_PAD), const),
            pl.BlockSpec((1, DT_PAD), const),
            pl.BlockSpec((1, D_INNER), const),
            pl.BlockSpec((1, D_INNER), const),
        ],
        out_specs=[
            pl.BlockSpec((TM_IN, CB_IN), lambda i, j: (i, jnp.clip(j - MAIN0, 0, MAIN_STEPS - 1))),
            pl.BlockSpec((TM_IN, CB_IN), lambda i, j: (i, jnp.maximum(j - QKV0, 0))),
            pl.BlockSpec((CHUNK, D_INNER),
                         lambda i, j: (i * CHUNKS_PER_TILE + jnp.clip(j - MAIN0, 0, CHUNKS_PER_TILE - 1), 0)),
        ],
        out_shape=[
            jax.ShapeDtypeStruct((t, N_MAIN), BF16),
            jax.ShapeDtypeStruct((t, N_QKV), BF16),
            jax.ShapeDtypeStruct((t, D_INNER), BF16),
        ],
        scratch_shapes=[
            pltpu.VMEM((3, TM_IN, D_MODEL), BF16),
            pltpu.VMEM((N_SLAB, TM_IN, LANES), F32),
            pltpu.VMEM((SSD_STEPS, TM_IN, CB_IN), BF16),
            pltpu.VMEM((TM_IN, DT_PAD), F32),
            pltpu.VMEM((N_PAIR, D_STATE, LANES), F32),
            pltpu.VMEM((CONV_SLABS, TAIL, LANES), F32),
            pltpu.VMEM((CONV_SLABS, TAIL + CHUNK, LANES), F32),
            pltpu.VMEM((CHUNK, CONV_DIM), F32),
        ],
        compiler_params=pltpu.CompilerParams(
            dimension_semantics=("arbitrary", "arbitrary"), vmem_limit_bytes=VMEM_LIMIT),
        name="inproj_ssd",
    )(x2, gain, w_all, w_dt, conv_w, conv_b, dt_bias, a_log, d_skip_x, ssd_norm)


def _memkv_kernel(m_ref, g_ref, w_ref, out_ref):
    m = m_ref[...]
    ms = jnp.mean(m * m, axis=-1, keepdims=True)
    h = (m * lax.rsqrt(ms + EPS) * g_ref[...]).astype(BF16)
    out_ref[...] = jnp.dot(h, w_ref[...], preferred_element_type=F32).astype(BF16)


def _memkv(mem2, gain, w_kv):
    rows = mem2.shape[0]
    return pl.pallas_call(
        _memkv_kernel,
        grid=(2 * MEM_W // CB,),
        in_specs=[
            pl.BlockSpec((rows, D_MODEL), lambda j: (0, 0)),
            pl.BlockSpec((1, D_MODEL), lambda j: (0, 0)),
            pl.BlockSpec((D_MODEL, CB), lambda j: (0, j)),
        ],
        out_specs=pl.BlockSpec((rows, CB), lambda j: (0, j)),
        out_shape=jax.ShapeDtypeStruct((rows, 2 * MEM_W), BF16),
        compiler_params=pltpu.CompilerParams(
            dimension_semantics=("arbitrary",), vmem_limit_bytes=VMEM_LIMIT),
        name="memkv",
    )(mem2, gain, w_kv)


def _make_attn_kernel(rows):
    parts = ATT_BLOCK // rows
    blocks = Q_STEP // ATT_BLOCK

    def load(ref, blk, sl):
        pieces = [ref[blk * parts + p, :, sl] for p in range(parts)]
        return pieces[0] if parts == 1 else jnp.concatenate(pieces, axis=0)

    def store(ref, blk, sl, val):
        for p in range(parts):
            ref[blk * parts + p, :, sl] = val[p * rows:(p + 1) * rows]

    def attn_kernel(q_ref, k_ref, v_ref, o_ref, lse_ref, kp_ref, vp_ref):
        n = pl.program_id(2)

        @pl.when(n == 0)
        def _():
            kp_ref[...] = jnp.zeros_like(kp_ref)
            vp_ref[...] = jnp.zeros_like(vp_ref)

        qi = lax.broadcasted_iota(jnp.int32, (ATT_BLOCK, 2 * ATT_BLOCK), 0)
        kj = lax.broadcasted_iota(jnp.int32, (ATT_BLOCK, 2 * ATT_BLOCK), 1)
        band = jnp.logical_and(kj >= qi, kj - ATT_BLOCK <= qi)
        lane = lax.broadcasted_iota(jnp.int32, (ATT_BLOCK, LANES), 1)
        first = lane < ATT_HEAD_DIM
        first_kv = lax.broadcasted_iota(jnp.int32, (2 * ATT_BLOCK, LANES), 1) < ATT_HEAD_DIM

        def block_body(blk, carry):
            first_key = jnp.where(n * blocks + blk > 0, 0, ATT_BLOCK)
            valid = jnp.logical_and(band, kj >= first_key)
            pairs, scores, probs, results, maxes = {}, {}, {}, {}, {}

            def load_pair(hp):
                sl = slice(hp * LANES, (hp + 1) * LANES)
                k_cur = load(k_ref, blk, sl)
                v_cur = load(v_ref, blk, sl)
                kk = jnp.concatenate([kp_ref[:, sl], k_cur], axis=0)
                vv = jnp.concatenate([vp_ref[:, sl], v_cur], axis=0)
                kp_ref[:, sl] = k_cur
                vp_ref[:, sl] = v_cur
                ones = jnp.ones_like(vv)
                v_ones = (jnp.where(first_kv, vv, ones), jnp.where(first_kv, ones, vv))
                pairs[hp] = (load(q_ref, blk, sl), kk, v_ones)

            def score(h):
                hp, e = divmod(h, 2)
                if e == 0:
                    load_pair(hp)
                q2, kk, _ = pairs[hp]
                qh = jnp.where(first if e == 0 else jnp.logical_not(first), q2, jnp.zeros_like(q2))
                scores[h] = lax.dot_general(qh, kk, (((1,), (1,)), ((), ())), preferred_element_type=F32)

            def softmax_numerator(h):
                sc = jnp.where(valid, scores.pop(h), NEG_INF)
                maxes[h] = jnp.max(sc, axis=-1, keepdims=True)
                probs[h] = jnp.exp2(sc - maxes[h]).astype(BF16)

            def weighted_values(h):
                hp, e = divmod(h, 2)
                results[h] = jnp.dot(probs.pop(h), pairs[hp][2][e], preferred_element_type=F32)

            def finish_pair(hp, lse_tile):
                r0, r1 = results.pop(2 * hp), results.pop(2 * hp + 1)
                den_other = jnp.where(first, r1, r0)
                den = pltpu.roll(den_other, ATT_HEAD_DIM, 1)
                o = jnp.where(first, r0, r1) / den
                store(o_ref, blk, slice(hp * LANES, (hp + 1) * LANES), o.astype(BF16))
                lse_tile = jnp.where(lane == 2 * hp, maxes.pop(2 * hp) + jnp.log2(den), lse_tile)
                lse_tile = jnp.where(lane == 2 * hp + 1, maxes.pop(2 * hp + 1) + jnp.log2(den_other), lse_tile)
                del pairs[hp]
                return lse_tile

            lse_tile = jnp.zeros((ATT_BLOCK, LANES), F32)
            for t in range(ATT_HEADS + 3):
                if t < ATT_HEADS:
                    score(t)
                if 0 <= t - 1 < ATT_HEADS:
                    softmax_numerator(t - 1)
                if 0 <= t - 2 < ATT_HEADS:
                    weighted_values(t - 2)
                if 0 <= t - 3 < ATT_HEADS and (t - 3) % 2 == 1:
                    lse_tile = finish_pair((t - 3) // 2, lse_tile)
            store(lse_ref, blk, slice(0, LANES), lse_tile)
            return carry

        lax.fori_loop(0, blocks, block_body, 0)

    return attn_kernel


def _attention(qkv2, bsz, seq, g):
    d = ATT_DILATIONS[g]
    rows = WIN[d] // d
    lead = (bsz, seq // WIN[d], d, rows)
    pieces = Q_STEP // rows
    view = qkv2.reshape(lead + (N_QKV,))
    blk = (None, pieces, None, rows, ATT_W)

    def in_spec(kind):
        return pl.BlockSpec(blk, lambda i, r, n: (i, n, r, 0, 3 * g + kind))

    out_map = lambda i, r, n: (i, n, r, 0, 0)
    return pl.pallas_call(
        _make_attn_kernel(rows),
        grid=(bsz, d, seq // d // Q_STEP),
        in_specs=[in_spec(kind) for kind in range(3)],
        out_specs=[pl.BlockSpec(blk, out_map), pl.BlockSpec(blk[:-1] + (LANES,), out_map)],
        out_shape=[
            jax.ShapeDtypeStruct(lead + (ATT_W,), BF16),
            jax.ShapeDtypeStruct(lead + (LANES,), F32),
        ],
        scratch_shapes=[
            pltpu.VMEM((ATT_BLOCK, ATT_W), BF16),
            pltpu.VMEM((ATT_BLOCK, ATT_W), BF16),
        ],
        compiler_params=pltpu.CompilerParams(
            dimension_semantics=("parallel", "parallel", "arbitrary"), vmem_limit_bytes=VMEM_LIMIT),
        name=f"attn_d{d}",
    )(view, view, view)


def _merge_kernel(x_ref, yssd_ref, o1_ref, o4_ref, o16_ref, l1_ref, l4_ref, l16_ref,
                  zatt_ref, qmem_ref, zmem_ref, g0_ref, g1_ref, g2_ref, kv_ref,
                  wso_ref, wao_ref, wmo_ref, wout_ref, npost_ref, p4_ref, p16_ref, ex_ref,
                  out_ref, tok_ref):
    n4, n16 = HALF_MERGE // 4, HALF_MERGE // 16

    def per_channel(wt):
        hi = wt.astype(BF16)
        lo = (wt - hi.astype(F32)).astype(BF16)
        return jnp.dot(jnp.concatenate([hi, lo], axis=1), ex_ref[...], preferred_element_type=F32)

    def steps(hf):
        rs = slice(hf * HALF_MERGE, (hf + 1) * HALF_MERGE)
        v = {}

        def weights():
            for r in range(4):
                tok_ref[2 * hf, pl.ds(r, n4, stride=4), :] = l4_ref[r, hf * n4:(hf + 1) * n4, :]
            for r in range(16):
                tok_ref[2 * hf + 1, pl.ds(r, n16, stride=16), :] = l16_ref[r, hf * n16:(hf + 1) * n16, :]
            la, lb, lc = l1_ref[rs, :], tok_ref[2 * hf], tok_ref[2 * hf + 1]
            top = jnp.maximum(la, jnp.maximum(lb, lc))
            ea, eb, ec = jnp.exp2(la - top), jnp.exp2(lb - top), jnp.exp2(lc - top)
            inv = 1.0 / (ea + eb + ec)
            v["w4"], v["w16"] = per_channel(eb * inv), per_channel(ec * inv)
            o4 = o4_ref[:, hf * n4:(hf + 1) * n4, :].reshape(HALF_MERGE, ATT_W)
            o16 = o16_ref[:, hf * n16:(hf + 1) * n16, :].reshape(HALF_MERGE, ATT_W)
            v["o4"] = jnp.dot(p4_ref[...], o4, preferred_element_type=F32)
            v["o16"] = jnp.dot(p16_ref[...], o16, preferred_element_type=F32)

        def combine():
            w4, w16 = v.pop("w4"), v.pop("w16")
            o = ((1.0 - w4 - w16) * o1_ref[rs, :].astype(F32) + w4 * v.pop("o4")
                 + w16 * v.pop("o16"))
            v["y_att"] = (o * _silu(zatt_ref[rs, :].astype(F32))).astype(BF16)

        def ssd_att_proj():
            v["ssd"] = jnp.dot(yssd_ref[rs, :], wso_ref[...], preferred_element_type=F32)
            v["att"] = jnp.dot(v.pop("y_att"), wao_ref[...], preferred_element_type=F32)

        def mem_attention():
            heads = []
            for h in range(MEM_HEADS):
                sl = slice(h * MEM_HEAD_DIM, (h + 1) * MEM_HEAD_DIM)
                q = qmem_ref[rs, sl] * (MEM_HEAD_DIM ** -0.5)
                k = kv_ref[0, :, sl]
                vv = kv_ref[0, :, MEM_W + h * MEM_HEAD_DIM:MEM_W + (h + 1) * MEM_HEAD_DIM]
                sc = lax.dot_general(q, k, (((1,), (1,)), ((), ())), preferred_element_type=F32)
                mx = jnp.max(sc, axis=-1, keepdims=True)
                p = jnp.exp(sc - mx)
                den = jnp.sum(p, axis=-1, keepdims=True)
                heads.append(jnp.dot(p.astype(BF16), vv, preferred_element_type=F32) / den)
            o_mem = jnp.concatenate(heads, axis=1)
            v["y_mem"] = (o_mem * _silu(zmem_ref[rs, :].astype(F32))).astype(BF16)

        def mem_proj():
            v["mem"] = jnp.dot(v.pop("y_mem"), wmo_ref[...], preferred_element_type=F32)

        def gates():
            merged = (_sigmoid(g0_ref[rs, :].astype(F32)) * v.pop("ssd")
                      + _sigmoid(g1_ref[rs, :].astype(F32)) * v.pop("att")
                      + _sigmoid(g2_ref[rs, :].astype(F32)) * v.pop("mem"))
            v["merged"] = merged.astype(BF16)

        def out_proj():
            v["out"] = jnp.dot(v.pop("merged"), wout_ref[...], preferred_element_type=F32)

        def finish():
            out = v.pop("out")
            ms = jnp.mean(out * out, axis=-1, keepdims=True)
            out_ref[rs, :] = x_ref[rs, :] + out * lax.rsqrt(ms + EPS) * npost_ref[...]

        return [weights, combine, ssd_att_proj, mem_attention, mem_proj, gates, out_proj, finish]

    first_half, second_half = steps(0), steps(1)
    for t in range(len(first_half) + 1):
        if t < len(first_half):
            first_half[t]()
        if t >= 1:
            second_half[t - 1]()


def _stream_to_token_matrix(d, tm):
    tok = jnp.arange(tm)
    src = (tok % d) * (tm // d) + tok // d
    return (src[:, None] == jnp.arange(tm)[None, :]).astype(BF16)


def _head_to_channel_matrix():
    lane = jnp.arange(2 * LANES) % LANES
    chan_head = jnp.arange(ATT_W) // ATT_HEAD_DIM
    return (lane[:, None] == chan_head[None, :]).astype(BF16)


def _merge(x2, y_ssd, o1, l1, o4, l4, o16, l16, main2, kv3, w_so, w_ao, w_mo, w_out, n_post, seq):
    t = x2.shape[0]
    tm = TM_MERGE
    per_batch = seq // tm
    row = lambda i: (i, 0)
    const = lambda i: (0, 0)

    def mblk(j):
        return pl.BlockSpec((tm, CB), lambda i: (i, j))

    def stream_spec(d, width):
        per_win = WIN[d] // tm
        return pl.BlockSpec((None, None, d, tm // d, width),
                            lambda i: (i // per_batch, (i % per_batch) // per_win, 0, i % per_win, 0))

    return pl.pallas_call(
        _merge_kernel,
        grid=(t // tm,),
        in_specs=[
            pl.BlockSpec((tm, D_MODEL), row),
            pl.BlockSpec((tm, D_INNER), row),
            pl.BlockSpec((tm, ATT_W), row), stream_spec(4, ATT_W), stream_spec(16, ATT_W),
            pl.BlockSpec((tm, LANES), row), stream_spec(4, LANES), stream_spec(16, LANES),
            mblk(BLK_ZATT), mblk(BLK_QMEM), mblk(BLK_ZMEM),
            mblk(BLK_GATE), mblk(BLK_GATE + 1), mblk(BLK_GATE + 2),
            pl.BlockSpec((1, MEM_LEN, 2 * MEM_W), lambda i: (i // per_batch, 0, 0)),
            pl.BlockSpec((D_INNER, D_MODEL), const),
            pl.BlockSpec((ATT_W, D_MODEL), const),
            pl.BlockSpec((MEM_W, D_MODEL), const),
            pl.BlockSpec((D_MODEL, D_MODEL), const),
            pl.BlockSpec((1, D_MODEL), const),
            pl.BlockSpec((HALF_MERGE, HALF_MERGE), const),
            pl.BlockSpec((HALF_MERGE, HALF_MERGE), const),
            pl.BlockSpec((2 * LANES, ATT_W), const),
        ],
        out_specs=pl.BlockSpec((tm, D_MODEL), row),
        out_shape=jax.ShapeDtypeStruct((t, D_MODEL), F32),
        scratch_shapes=[pltpu.VMEM((4, HALF_MERGE, LANES), F32)],
        compiler_params=pltpu.CompilerParams(
            dimension_semantics=("parallel",), vmem_limit_bytes=VMEM_LIMIT),
        name="merge",
    )(x2, y_ssd, o1, o4, o16, l1, l4, l16, main2, main2, main2, main2, main2, main2, kv3,
      w_so, w_ao, w_mo, w_out, n_post, _stream_to_token_matrix(4, HALF_MERGE),
      _stream_to_token_matrix(16, HALF_MERGE),
      _head_to_channel_matrix())


def _pad_lanes(v, width):
    return jnp.pad(v, ((0, 0), (0, width - v.shape[-1])))


def _regroup_w_in(w):
    cols = [w[:, :OFF_DT], w[:, OFF_ZATT:]]
    for g in range(ATT_GROUPS):
        for kind in range(3):
            start = OFF_QKV + (kind * ATT_GROUPS + g) * ATT_W
            sec = w[:, start:start + ATT_W]
            cols.append(sec * Q_SCALE if kind == 0 else sec)
    return jnp.concatenate(cols, axis=1).astype(BF16)


def kernel(x, mem, norm_pre, norm_post, w_in, conv_w, conv_b, dt_bias, a_log, d_skip, ssd_norm,
           w_ssd_out, w_attn_out, mem_norm, w_mem_kv, w_mem_out, w_out):
    bsz, seq, _ = x.shape
    depth = w_in.shape[0]
    t = bsz * seq
    assert seq % (16 * Q_STEP) == 0 and seq % TM_IN == 0 and WIN[4] % TM_MERGE == 0

    x2 = x.reshape(t, D_MODEL)
    mem2 = mem.reshape(bsz * MEM_LEN, D_MODEL)
    for l in range(depth):
        w = w_in[l]
        w_dt = _pad_lanes(w[:, OFF_DT:OFF_QKV], DT_PAD).astype(BF16)
        main2, qkv2, y_ssd = _inproj_ssd(
            x2, norm_pre[l][None, :], _regroup_w_in(w), w_dt, conv_w[l], conv_b[l][None, :],
            _pad_lanes(dt_bias[l][None, :], DT_PAD), _pad_lanes(a_log[l][None, :], DT_PAD),
            jnp.repeat(d_skip[l], SSD_HEAD_DIM)[None, :], ssd_norm[l][None, :], seq)
        kv = _memkv(mem2, mem_norm[l][None, :], w_mem_kv[l].astype(BF16))

        o1, l1 = _attention(qkv2, bsz, seq, 0)
        o4, l4 = _attention(qkv2, bsz, seq, 1)
        o16, l16 = _attention(qkv2, bsz, seq, 2)

        x2 = _merge(x2, y_ssd, o1.reshape(t, ATT_W), l1.reshape(t, LANES),
                    o4, l4, o16, l16, main2, kv.reshape(bsz, MEM_LEN, 2 * MEM_W),
                    w_ssd_out[l].astype(BF16), w_attn_out[l].astype(BF16),
                    w_mem_out[l].astype(BF16), w_out[l].astype(BF16), norm_post[l][None, :], seq)
    return x2.reshape(bsz, seq, D_MODEL)
```

```python
import math

import jax
import jax.numpy as jnp
from jax import lax
from jax.experimental import pallas as pl
from jax.experimental.pallas import tpu as pltpu

F32 = jnp.float32
BF16 = jnp.bfloat16

EPS = 1e-6
D_MODEL = 1024

D_INNER = 2048
SSD_HEAD_DIM = 64
SSD_HEADS = 32
SSD_GROUPS = 8
D_STATE = 128
CONV_K = 4
CONV_DIM = D_INNER + 2 * SSD_GROUPS * D_STATE
CHUNK = 128

ATT_DILATIONS = (1, 4, 16)
ATT_GROUPS = 3
ATT_HEADS = 16
ATT_HEAD_DIM = 64
ATT_W = 1024
ATT_BLOCK = 128

MEM_LEN = 256
MEM_HEADS = 4
MEM_HEAD_DIM = 256
MEM_W = 1024

OFF_XBC = D_INNER
OFF_DT = OFF_XBC + CONV_DIM
OFF_QKV = OFF_DT + SSD_HEADS
OFF_ZATT = OFF_QKV + 3 * ATT_GROUPS * ATT_W

CB = 1024
BLK_ZATT = 0
BLK_QMEM = 1
BLK_ZMEM = 2
BLK_GATE = 3
N_SSD_IN = D_INNER + CONV_DIM
N_MAIN = 6 * CB
N_QKV = 3 * ATT_GROUPS * ATT_W
CB_IN = 1536
SSD_STEPS = N_SSD_IN // CB_IN
MAIN_STEPS = N_MAIN // CB_IN
GROUP_STEPS = 3 * ATT_W // CB_IN
MAIN0 = SSD_STEPS
QKV0 = SSD_STEPS + MAIN_STEPS
N_STEPS = QKV0 + ATT_GROUPS * GROUP_STEPS

LANES = 128
N_SLAB = D_MODEL // LANES
DT_PAD = LANES

TM_IN = 1024
TM_MERGE = 512
HALF_MERGE = TM_MERGE // 2
WIN = {1: ATT_BLOCK, 4: 4 * ATT_BLOCK, 16: TM_IN}
Q_STEP = 4 * ATT_BLOCK

VMEM_LIMIT = 62 * 1024 * 1024

NEG_INF = float("-inf")
LOG2E = math.log2(math.e)
Q_SCALE = LOG2E * ATT_HEAD_DIM ** -0.5


def _sigmoid(v):
    return 0.5 * jnp.tanh(0.5 * v) + 0.5


def _silu(v):
    return v * _sigmoid(v)


def _split_dot(lhs_bf16, v):
    hi = v.astype(BF16)
    r1 = v - hi.astype(F32)
    mid = r1.astype(BF16)
    lo = (r1 - mid.astype(F32)).astype(BF16)
    acc = jnp.dot(lhs_bf16, hi, preferred_element_type=F32)
    acc = acc + jnp.dot(lhs_bf16, mid, preferred_element_type=F32)
    return acc + jnp.dot(lhs_bf16, lo, preferred_element_type=F32)


N_PAIR = SSD_HEADS // 2
TAIL = 8
CONV_SLABS = CONV_DIM // LANES
CHUNKS_PER_TILE = TM_IN // CHUNK
MM_PIECE = 256
CONV_SLOTS = 4
CONV_SLABS_PER_SLOT = CONV_SLABS // CONV_SLOTS
N_SLOTS = CONV_SLOTS + SSD_GROUPS


def _ssd_chunk(slab, dt_rows, cw_ref, cb_ref, dtb_ref, alog_ref, dskip_ref, nrm_ref, y_ref,
               state_ref, tail_ref, cbuf_ref, act_ref, interleave):
    for s in range(CONV_SLABS):
        if s % CONV_SLABS_PER_SLOT == 0:
            interleave(s // CONV_SLABS_PER_SLOT)
        sl = slice(s * LANES, (s + 1) * LANES)
        cbuf_ref[s, 0:TAIL, :] = tail_ref[s]
        cbuf_ref[s, TAIL:TAIL + CHUNK, :] = slab(D_INNER + s * LANES).astype(F32)
        conv = cb_ref[:, sl]
        for k in range(CONV_K):
            lo = TAIL - (CONV_K - 1) + k
            conv = conv + cw_ref[k:k + 1, sl] * cbuf_ref[s, pl.ds(lo, CHUNK, stride=1), :]
        tail_ref[s] = cbuf_ref[s, CHUNK:CHUNK + TAIL, :]
        act_ref[:, sl] = _silu(conv)

    v = dt_rows + dtb_ref[...]
    dt = jnp.maximum(v, 0.0) + jnp.log1p(jnp.exp(-jnp.abs(v)))
    a2 = -jnp.exp(alog_ref[...]) * LOG2E
    row = lax.broadcasted_iota(jnp.int32, (CHUNK, CHUNK), 0)
    col = lax.broadcasted_iota(jnp.int32, (CHUNK, CHUNK), 1)
    causal = row >= col
    tri = jnp.where(causal, 1.0, 0.0).astype(BF16)
    acum = _split_dot(tri, dt * a2)
    total = acum[CHUNK - 1:CHUNK, :]
    acum_t = acum.T
    dt_t = dt.T
    dsts_t = (jnp.exp2(total - acum) * dt).T
    cdec = jnp.exp2(total)

    lane = lax.broadcasted_iota(jnp.int32, (CHUNK, LANES), 1)
    first = lane < SSD_HEAD_DIM

    for g in range(SSD_GROUPS):
        interleave(CONV_SLOTS + g)
        b_f = act_ref[:, D_INNER + g * D_STATE:D_INNER + (g + 1) * D_STATE]
        c_f = act_ref[:, D_INNER + (SSD_GROUPS + g) * D_STATE:D_INNER + (SSD_GROUPS + g + 1) * D_STATE]
        b_t = b_f.T
        cbm = lax.dot_general(c_f.astype(BF16), b_f.astype(BF16), (((1,), (1,)), ((), ())),
                              preferred_element_type=F32)
        y_pairs = []
        for pp in range(2):
            pair = 2 * g + pp
            x_pair = act_ref[:, pair * LANES:(pair + 1) * LANES]
            s_prev = state_ref[pair]
            rhs = jnp.concatenate([x_pair, s_prev], axis=0).astype(BF16)
            x_b = x_pair.astype(BF16)
            ys, sts, decs = [], [], []
            for e in range(2):
                h = 2 * pair + e
                acb = jnp.broadcast_to(acum[:, h:h + 1], (CHUNK, CHUNK))
                seg = acb - acum_t[h:h + 1, :]
                decay = jnp.exp2(jnp.where(causal, seg, NEG_INF))
                m = cbm * decay * dt_t[h:h + 1, :]
                c_exp = c_f * jnp.exp2(acb)
                lhs = jnp.concatenate([m, c_exp], axis=1).astype(BF16)
                ys.append(jnp.dot(lhs, rhs, preferred_element_type=F32))
                bts = (b_t * dsts_t[h:h + 1, :]).astype(BF16)
                sts.append(jnp.dot(bts, x_b, preferred_element_type=F32))
                decs.append(jnp.broadcast_to(cdec[:, h:h + 1], (CHUNK, LANES)))
            state_ref[pair] = (s_prev * jnp.where(first, decs[0], decs[1])
                               + jnp.where(first, sts[0], sts[1]))
            y_pair = jnp.where(first, ys[0], ys[1])
            y_pairs.append(y_pair + dskip_ref[:, pair * LANES:(pair + 1) * LANES] * x_pair)

        gsl = slice(2 * g * LANES, (2 * g + 2) * LANES)
        z = jnp.concatenate([slab(2 * g * LANES), slab((2 * g + 1) * LANES)], axis=1).astype(F32)
        yg = jnp.concatenate(y_pairs, axis=1) * _silu(z)
        ms = jnp.mean(yg * yg, axis=-1, keepdims=True)
        y_ref[:, gsl] = (yg * lax.rsqrt(ms + EPS) * nrm_ref[:, gsl]).astype(BF16)


def _make_inproj_ssd_kernel(tiles_per_batch):
    def kernel_body(x_ref, g_ref, w_ref, wdt_ref, cw_ref, cb_ref, dtb_ref, alog_ref, dskip_ref, nrm_ref,
                    main_ref, qkv_ref, y_ref,
                    h_ref, hs_ref, zx_ref, dt_ref, state_ref, tail_ref, cbuf_ref, act_ref):
        i = pl.program_id(0)
        j = pl.program_id(1)

        @pl.when(j == 0)
        def _():
            x = x_ref[...]
            ms = jnp.mean(x * x, axis=-1, keepdims=True)
            h = x * lax.rsqrt(ms + EPS) * g_ref[...]
            hb = h.astype(BF16)
            h_ref[0] = hb
            dt_ref[...] = jnp.dot(hb, wdt_ref[...], preferred_element_type=F32)
            rows16 = TM_IN // 16
            for s in range(N_SLAB):
                sl = slice(s * LANES, (s + 1) * LANES)
                hs_ref[s] = h[:, sl]
                for w in range(TM_IN // WIN[4]):
                    for r in range(4):
                        dst = w * WIN[4] + r * ATT_BLOCK
                        h_ref[1, dst:dst + ATT_BLOCK, sl] = (
                            hs_ref[s, pl.ds(w * WIN[4] + r, ATT_BLOCK, stride=4), :].astype(BF16))
                for r in range(16):
                    h_ref[2, r * rows16:(r + 1) * rows16, sl] = (
                        hs_ref[s, pl.ds(r, rows16, stride=16), :].astype(BF16))

            @pl.when(i % tiles_per_batch == 0)
            def _():
                state_ref[...] = jnp.zeros_like(state_ref)
                tail_ref[...] = jnp.zeros_like(tail_ref)

        def project(which, out_ref, cols):
            out_ref[:, cols] = jnp.dot(h_ref[which], w_ref[:, cols],
                                       preferred_element_type=F32).astype(BF16)

        def with_ssd_chunk(which, out_ref):
            r0 = pl.multiple_of((j - MAIN0) * CHUNK, CHUNK)

            def slab(c):
                return zx_ref[c // CB_IN, pl.ds(r0, CHUNK), c % CB_IN:c % CB_IN + LANES]

            def interleave(slot):
                k = slot // 2
                if slot % 2 == 0 and k < CB_IN // MM_PIECE:
                    project(which, out_ref, slice(k * MM_PIECE, (k + 1) * MM_PIECE))

            _ssd_chunk(slab, dt_ref[pl.ds(r0, CHUNK), :], cw_ref, cb_ref, dtb_ref, alog_ref, dskip_ref,
                       nrm_ref, y_ref, state_ref, tail_ref, cbuf_ref, act_ref, interleave)

        group = jnp.maximum(j - QKV0, 0) // GROUP_STEPS

        @pl.when(j < MAIN0)
        def _():
            zx_ref[j] = jnp.dot(h_ref[0], w_ref[...], preferred_element_type=F32).astype(BF16)

        @pl.when(jnp.logical_and(j >= MAIN0, j < QKV0))
        def _():
            with_ssd_chunk(0, main_ref)

        @pl.when(jnp.logical_and(j >= QKV0, j < MAIN0 + CHUNKS_PER_TILE))
        def _():
            with_ssd_chunk(group, qkv_ref)

        @pl.when(j >= MAIN0 + CHUNKS_PER_TILE)
        def _():
            project(group, qkv_ref, slice(0, CB_IN))

    return kernel_body


def _inproj_ssd(x2, gain, w_all, w_dt, conv_w, conv_b, dt_bias, a_log, d_skip_x, ssd_norm, seq):
    t = x2.shape[0]
    const = lambda i, j: (0, 0)
    return pl.pallas_call(
        _make_inproj_ssd_kernel(seq // TM_IN),
        grid=(t // TM_IN, N_STEPS),
        in_specs=[
            pl.BlockSpec((TM_IN, D_MODEL), lambda i, j: (i, 0)),
            pl.BlockSpec((1, D_MODEL), const),
            pl.BlockSpec((D_MODEL, CB_IN), lambda i, j: (0, j)),
            pl.BlockSpec((D_MODEL, DT_PAD), const),
            pl.BlockSpec((CONV_K, CONV_DIM), const),
            pl.BlockSpec((1, CONV_DIM), const),
            pl.BlockSpec((1, DT_PAD), const),
            pl.BlockSpec((1, DT_PAD), const),
            pl.BlockSpec((1, D_INNER), const),
            pl.BlockSpec((1, D_INNER), const),
        ],
        out_specs=[
            pl.BlockSpec((TM_IN, CB_IN), lambda i, j: (i, jnp.clip(j - MAIN0, 0, MAIN_STEPS - 1))),
            pl.BlockSpec((TM_IN, CB_IN), lambda i, j: (i, jnp.maximum(j - QKV0, 0))),
            pl.BlockSpec((CHUNK, D_INNER),
                         lambda i, j: (i * CHUNKS_PER_TILE + jnp.clip(j - MAIN0, 0, CHUNKS_PER_TILE - 1), 0)),
        ],
        out_shape=[
            jax.ShapeDtypeStruct((t, N_MAIN), BF16),
            jax.ShapeDtypeStruct((t, N_QKV), BF16),
            jax.ShapeDtypeStruct((t, D_INNER), BF16),
        ],
        scratch_shapes=[
            pltpu.VMEM((3, TM_IN, D_MODEL), BF16),
            pltpu.VMEM((N_SLAB, TM_IN, LANES), F32),
            pltpu.VMEM((SSD_STEPS, TM_IN, CB_IN), BF16),
            pltpu.VMEM((TM_IN, DT_PAD), F32),
            pltpu.VMEM((N_PAIR, D_STATE, LANES), F32),
            pltpu.VMEM((CONV_SLABS, TAIL, LANES), F32),
            pltpu.VMEM((CONV_SLABS, TAIL + CHUNK, LANES), F32),
            pltpu.VMEM((CHUNK, CONV_DIM), F32),
        ],
        compiler_params=pltpu.CompilerParams(
            dimension_semantics=("arbitrary", "arbitrary"), vmem_limit_bytes=VMEM_LIMIT),
        name="inproj_ssd",
    )(x2, gain, w_all, w_dt, conv_w, conv_b, dt_bias, a_log, d_skip_x, ssd_norm)


def _memkv_kernel(m_ref, g_ref, w_ref, out_ref):
    m = m_ref[...]
    ms = jnp.mean(m * m, axis=-1, keepdims=True)
    h = (m * lax.rsqrt(ms + EPS) * g_ref[...]).astype(BF16)
    out_ref[...] = jnp.dot(h, w_ref[...], preferred_element_type=F32).astype(BF16)


def _memkv(mem2, gain, w_kv):
    rows = mem2.shape[0]
    return pl.pallas_call(
        _memkv_kernel,
        grid=(2 * MEM_W // CB,),
        in_specs=[
            pl.BlockSpec((rows, D_MODEL), lambda j: (0, 0)),
            pl.BlockSpec((1, D_MODEL), lambda j: (0, 0)),
            pl.BlockSpec((D_MODEL, CB), lambda j: (0, j)),
        ],
        out_specs=pl.BlockSpec((rows, CB), lambda j: (0, j)),
        out_shape=jax.ShapeDtypeStruct((rows, 2 * MEM_W), BF16),
        compiler_params=pltpu.CompilerParams(
            dimension_semantics=("arbitrary",), vmem_limit_bytes=VMEM_LIMIT),
        name="memkv",
    )(mem2, gain, w_kv)


def _make_attn_kernel(rows):
    parts = ATT_BLOCK // rows
    blocks = Q_STEP // ATT_BLOCK

    def load(ref, blk, sl):
        pieces = [ref[blk * parts + p, :, sl] for p in range(parts)]
        return pieces[0] if parts == 1 else jnp.concatenate(pieces, axis=0)

    def store(ref, blk, sl, val):
        for p in range(parts):
            ref[blk * parts + p, :, sl] = val[p * rows:(p + 1) * rows]

    def attn_kernel(q_ref, k_ref, v_ref, o_ref, lse_ref, kp_ref, vp_ref):
        n = pl.program_id(2)

        @pl.when(n == 0)
        def _():
            kp_ref[...] = jnp.zeros_like(kp_ref)
            vp_ref[...] = jnp.zeros_like(vp_ref)

        qi = lax.broadcasted_iota(jnp.int32, (ATT_BLOCK, 2 * ATT_BLOCK), 0)
        kj = lax.broadcasted_iota(jnp.int32, (ATT_BLOCK, 2 * ATT_BLOCK), 1)
        band = jnp.logical_and(kj >= qi, kj - ATT_BLOCK <= qi)
        lane = lax.broadcasted_iota(jnp.int32, (ATT_BLOCK, LANES), 1)
        first = lane < ATT_HEAD_DIM
        first_kv = lax.broadcasted_iota(jnp.int32, (2 * ATT_BLOCK, LANES), 1) < ATT_HEAD_DIM

        def block_body(blk, carry):
            first_key = jnp.where(n * blocks + blk > 0, 0, ATT_BLOCK)
            valid = jnp.logical_and(band, kj >= first_key)
            pairs, scores, probs, results, maxes = {}, {}, {}, {}, {}

            def load_pair(hp):
                sl = slice(hp * LANES, (hp + 1) * LANES)
                k_cur = load(k_ref, blk, sl)
                v_cur = load(v_ref, blk, sl)
                kk = jnp.concatenate([kp_ref[:, sl], k_cur], axis=0)
                vv = jnp.concatenate([vp_ref[:, sl], v_cur], axis=0)
                kp_ref[:, sl] = k_cur
                vp_ref[:, sl] = v_cur
                ones = jnp.ones_like(vv)
                v_ones = (jnp.where(first_kv, vv, ones), jnp.where(first_kv, ones, vv))
                pairs[hp] = (load(q_ref, blk, sl), kk, v_ones)

            def score(h):
                hp, e = divmod(h, 2)
                if e == 0:
                    load_pair(hp)
                q2, kk, _ = pairs[hp]
                qh = jnp.where(first if e == 0 else jnp.logical_not(first), q2, jnp.zeros_like(q2))
                scores[h] = lax.dot_general(qh, kk, (((1,), (1,)), ((), ())), preferred_element_type=F32)

            def softmax_numerator(h):
                sc = jnp.where(valid, scores.pop(h), NEG_INF)
                maxes[h] = jnp.max(sc, axis=-1, keepdims=True)
                probs[h] = jnp.exp2(sc - maxes[h]).astype(BF16)

            def weighted_values(h):
                hp, e = divmod(h, 2)
                results[h] = jnp.dot(probs.pop(h), pairs[hp][2][e], preferred_element_type=F32)

            def finish_pair(hp, lse_tile):
                r0, r1 = results.pop(2 * hp), results.pop(2 * hp + 1)
                den_other = jnp.where(first, r1, r0)
                den = pltpu.roll(den_other, ATT_HEAD_DIM, 1)
                o = jnp.where(first, r0, r1) / den
                store(o_ref, blk, slice(hp * LANES, (hp + 1) * LANES), o.astype(BF16))
                lse_tile = jnp.where(lane == 2 * hp, maxes.pop(2 * hp) + jnp.log2(den), lse_tile)
                lse_tile = jnp.where(lane == 2 * hp + 1, maxes.pop(2 * hp + 1) + jnp.log2(den_other), lse_tile)
                del pairs[hp]
                return lse_tile

            lse_tile = jnp.zeros((ATT_BLOCK, LANES), F32)
            for t in range(ATT_HEADS + 3):
                if t < ATT_HEADS:
                    score(t)
                if 0 <= t - 1 < ATT_HEADS:
                    softmax_numerator(t - 1)
                if 0 <= t - 2 < ATT_HEADS:
                    weighted_values(t - 2)
                if 0 <= t - 3 < ATT_HEADS and (t - 3) % 2 == 1:
                    lse_tile = finish_pair((t - 3) // 2, lse_tile)
            store(lse_ref, blk, slice(0, LANES), lse_tile)
            return carry

        lax.fori_loop(0, blocks, block_body, 0)

    return attn_kernel


def _attention(qkv2, bsz, seq, g):
    d = ATT_DILATIONS[g]
    rows = WIN[d] // d
    lead = (bsz, seq // WIN[d], d, rows)
    pieces = Q_STEP // rows
    view = qkv2.reshape(lead + (N_QKV,))
    blk = (None, pieces, None, rows, ATT_W)

    def in_spec(kind):
        return pl.BlockSpec(blk, lambda i, r, n: (i, n, r, 0, 3 * g + kind))

    out_map = lambda i, r, n: (i, n, r, 0, 0)
    return pl.pallas_call(
        _make_attn_kernel(rows),
        grid=(bsz, d, seq // d // Q_STEP),
        in_specs=[in_spec(kind) for kind in range(3)],
        out_specs=[pl.BlockSpec(blk, out_map), pl.BlockSpec(blk[:-1] + (LANES,), out_map)],
        out_shape=[
            jax.ShapeDtypeStruct(lead + (ATT_W,), BF16),
            jax.ShapeDtypeStruct(lead + (LANES,), F32),
        ],
        scratch_shapes=[
            pltpu.VMEM((ATT_BLOCK, ATT_W), BF16),
            pltpu.VMEM((ATT_BLOCK, ATT_W), BF16),
        ],
        compiler_params=pltpu.CompilerParams(
            dimension_semantics=("parallel", "parallel", "arbitrary"), vmem_limit_bytes=VMEM_LIMIT),
        name=f"attn_d{d}",
    )(view, view, view)


def _merge_kernel(x_ref, yssd_ref, o1_ref, o4_ref, o16_ref, l1_ref, l4_ref, l16_ref,
                  zatt_ref, qmem_ref, zmem_ref, g0_ref, g1_ref, g2_ref, kv_ref,
                  wso_ref, wao_ref, wmo_ref, wout_ref, npost_ref, p4_ref, p16_ref, ex_ref,
                  out_ref, tok_ref):
    n4, n16 = HALF_MERGE // 4, HALF_MERGE // 16

    def per_channel(wt):
        hi = wt.astype(BF16)
        lo = (wt - hi.astype(F32)).astype(BF16)
        return jnp.dot(jnp.concatenate([hi, lo], axis=1), ex_ref[...], preferred_element_type=F32)

    def steps(hf):
        rs = slice(hf * HALF_MERGE, (hf + 1) * HALF_MERGE)
        v = {}

        def weights():
            for r in range(4):
                tok_ref[2 * hf, pl.ds(r, n4, stride=4), :] = l4_ref[r, hf * n4:(hf + 1) * n4, :]
            for r in range(16):
                tok_ref[2 * hf + 1, pl.ds(r, n16, stride=16), :] = l16_ref[r, hf * n16:(hf + 1) * n16, :]
            la, lb, lc = l1_ref[rs, :], tok_ref[2 * hf], tok_ref[2 * hf + 1]
            top = jnp.maximum(la, jnp.maximum(lb, lc))
            ea, eb, ec = jnp.exp2(la - top), jnp.exp2(lb - top), jnp.exp2(lc - top)
            inv = 1.0 / (ea + eb + ec)
            v["w4"], v["w16"] = per_channel(eb * inv), per_channel(ec * inv)
            o4 = o4_ref[:, hf * n4:(hf + 1) * n4, :].reshape(HALF_MERGE, ATT_W)
            o16 = o16_ref[:, hf * n16:(hf + 1) * n16, :].reshape(HALF_MERGE, ATT_W)
            v["o4"] = jnp.dot(p4_ref[...], o4, preferred_element_type=F32)
            v["o16"] = jnp.dot(p16_ref[...], o16, preferred_element_type=F32)

        def combine():
            w4, w16 = v.pop("w4"), v.pop("w16")
            o = ((1.0 - w4 - w16) * o1_ref[rs, :].astype(F32) + w4 * v.pop("o4")
                 + w16 * v.pop("o16"))
            v["y_att"] = (o * _silu(zatt_ref[rs, :].astype(F32))).astype(BF16)

        def ssd_att_proj():
            v["ssd"] = jnp.dot(yssd_ref[rs, :], wso_ref[...], preferred_element_type=F32)
            v["att"] = jnp.dot(v.pop("y_att"), wao_ref[...], preferred_element_type=F32)

        def mem_attention():
            heads = []
            for h in range(MEM_HEADS):
                sl = slice(h * MEM_HEAD_DIM, (h + 1) * MEM_HEAD_DIM)
                q = qmem_ref[rs, sl] * (MEM_HEAD_DIM ** -0.5)
                k = kv_ref[0, :, sl]
                vv = kv_ref[0, :, MEM_W + h * MEM_HEAD_DIM:MEM_W + (h + 1) * MEM_HEAD_DIM]
                sc = lax.dot_general(q, k, (((1,), (1,)), ((), ())), preferred_element_type=F32)
                mx = jnp.max(sc, axis=-1, keepdims=True)
                p = jnp.exp(sc - mx)
                den = jnp.sum(p, axis=-1, keepdims=True)
                heads.append(jnp.dot(p.astype(BF16), vv, preferred_element_type=F32) / den)
            o_mem = jnp.concatenate(heads, axis=1)
            v["y_mem"] = (o_mem * _silu(zmem_ref[rs, :].astype(F32))).astype(BF16)

        def mem_proj():
            v["mem"] = jnp.dot(v.pop("y_mem"), wmo_ref[...], preferred_element_type=F32)

        def gates():
            merged = (_sigmoid(g0_ref[rs, :].astype(F32)) * v.pop("ssd")
                      + _sigmoid(g1_ref[rs, :].astype(F32)) * v.pop("att")
                      + _sigmoid(g2_ref[rs, :].astype(F32)) * v.pop("mem"))
            v["merged"] = merged.astype(BF16)

        def out_proj():
            v["out"] = jnp.dot(v.pop("merged"), wout_ref[...], preferred_element_type=F32)

        def finish():
            out = v.pop("out")
            ms = jnp.mean(out * out, axis=-1, keepdims=True)
            out_ref[rs, :] = x_ref[rs, :] + out * lax.rsqrt(ms + EPS) * npost_ref[...]

        return [weights, combine, ssd_att_proj, mem_attention, mem_proj, gates, out_proj, finish]

    first_half, second_half = steps(0), steps(1)
    for t in range(len(first_half) + 1):
        if t < len(first_half):
            first_half[t]()
        if t >= 1:
            second_half[t - 1]()


def _stream_to_token_matrix(d, tm):
    tok = jnp.arange(tm)
    src = (tok % d) * (tm // d) + tok // d
    return (src[:, None] == jnp.arange(tm)[None, :]).astype(BF16)


def _head_to_channel_matrix():
    lane = jnp.arange(2 * LANES) % LANES
    chan_head = jnp.arange(ATT_W) // ATT_HEAD_DIM
    return (lane[:, None] == chan_head[None, :]).astype(BF16)


def _merge(x2, y_ssd, o1, l1, o4, l4, o16, l16, main2, kv3, w_so, w_ao, w_mo, w_out, n_post, seq):
    t = x2.shape[0]
    tm = TM_MERGE
    per_batch = seq // tm
    row = lambda i: (i, 0)
    const = lambda i: (0, 0)

    def mblk(j):
        return pl.BlockSpec((tm, CB), lambda i: (i, j))

    def stream_spec(d, width):
        per_win = WIN[d] // tm
        return pl.BlockSpec((None, None, d, tm // d, width),
                            lambda i: (i // per_batch, (i % per_batch) // per_win, 0, i % per_win, 0))

    return pl.pallas_call(
        _merge_kernel,
        grid=(t // tm,),
        in_specs=[
            pl.BlockSpec((tm, D_MODEL), row),
            pl.BlockSpec((tm, D_INNER), row),
            pl.BlockSpec((tm, ATT_W), row), stream_spec(4, ATT_W), stream_spec(16, ATT_W),
            pl.BlockSpec((tm, LANES), row), stream_spec(4, LANES), stream_spec(16, LANES),
            mblk(BLK_ZATT), mblk(BLK_QMEM), mblk(BLK_ZMEM),
            mblk(BLK_GATE), mblk(BLK_GATE + 1), mblk(BLK_GATE + 2),
            pl.BlockSpec((1, MEM_LEN, 2 * MEM_W), lambda i: (i // per_batch, 0, 0)),
            pl.BlockSpec((D_INNER, D_MODEL), const),
            pl.BlockSpec((ATT_W, D_MODEL), const),
            pl.BlockSpec((MEM_W, D_MODEL), const),
            pl.BlockSpec((D_MODEL, D_MODEL), const),
            pl.BlockSpec((1, D_MODEL), const),
            pl.BlockSpec((HALF_MERGE, HALF_MERGE), const),
            pl.BlockSpec((HALF_MERGE, HALF_MERGE), const),
            pl.BlockSpec((2 * LANES, ATT_W), const),
        ],
        out_specs=pl.BlockSpec((tm, D_MODEL), row),
        out_shape=jax.ShapeDtypeStruct((t, D_MODEL), F32),
        scratch_shapes=[pltpu.VMEM((4, HALF_MERGE, LANES), F32)],
        compiler_params=pltpu.CompilerParams(
            dimension_semantics=("parallel",), vmem_limit_bytes=VMEM_LIMIT),
        name="merge",
    )(x2, y_ssd, o1, o4, o16, l1, l4, l16, main2, main2, main2, main2, main2, main2, kv3,
      w_so, w_ao, w_mo, w_out, n_post, _stream_to_token_matrix(4, HALF_MERGE),
      _stream_to_token_matrix(16, HALF_MERGE),
      _head_to_channel_matrix())


def _pad_lanes(v, width):
    return jnp.pad(v, ((0, 0), (0, width - v.shape[-1])))


def _regroup_w_in(w):
    col = jnp.arange(w.shape[1])
    is_q = jnp.logical_and(col >= OFF_QKV, col < OFF_QKV + ATT_GROUPS * ATT_W)
    w = w * jnp.where(is_q, Q_SCALE, 1.0).astype(F32)[None, :]
    cols = [w[:, :OFF_DT], w[:, OFF_ZATT:]]
    for g in range(ATT_GROUPS):
        for kind in range(3):
            start = OFF_QKV + (kind * ATT_GROUPS + g) * ATT_W
            cols.append(w[:, start:start + ATT_W])
    return jnp.concatenate(cols, axis=1).astype(BF16)


def kernel(x, mem, norm_pre, norm_post, w_in, conv_w, conv_b, dt_bias, a_log, d_skip, ssd_norm,
           w_ssd_out, w_attn_out, mem_norm, w_mem_kv, w_mem_out, w_out):
    bsz, seq, _ = x.shape
    depth = w_in.shape[0]
    t = bsz * seq
    assert seq % (16 * Q_STEP) == 0 and seq % TM_IN == 0 and WIN[4] % TM_MERGE == 0

    x2 = x.reshape(t, D_MODEL)
    mem2 = mem.reshape(bsz * MEM_LEN, D_MODEL)
    for l in range(depth):
        w = w_in[l]
        w_dt = _pad_lanes(w[:, OFF_DT:OFF_QKV], DT_PAD).astype(BF16)
        main2, qkv2, y_ssd = _inproj_ssd(
            x2, norm_pre[l][None, :], _regroup_w_in(w), w_dt, conv_w[l], conv_b[l][None, :],
            _pad_lanes(dt_bias[l][None, :], DT_PAD), _pad_lanes(a_log[l][None, :], DT_PAD),
            jnp.repeat(d_skip[l], SSD_HEAD_DIM)[None, :], ssd_norm[l][None, :], seq)
        kv = _memkv(mem2, mem_norm[l][None, :], w_mem_kv[l].astype(BF16))

        o1, l1 = _attention(qkv2, bsz, seq, 0)
        o4, l4 = _attention(qkv2, bsz, seq, 1)
        o16, l16 = _attention(qkv2, bsz, seq, 2)

        x2 = _merge(x2, y_ssd, o1.reshape(t, ATT_W), l1.reshape(t, LANES),
                    o4, l4, o16, l16, main2, kv.reshape(bsz, MEM_LEN, 2 * MEM_W),
                    w_ssd_out[l].astype(BF16), w_attn_out[l].astype(BF16),
                    w_mem_out[l].astype(BF16), w_out[l].astype(BF16), norm_post[l][None, :], seq)
    return x2.reshape(bsz, seq, D_MODEL)
```

```python
import math

import jax
import jax.numpy as jnp
from jax import lax
from jax.experimental import pallas as pl
from jax.experimental.pallas import tpu as pltpu

F32 = jnp.float32
BF16 = jnp.bfloat16

EPS = 1e-6
D_MODEL = 1024

D_INNER = 2048
SSD_HEAD_DIM = 64
SSD_HEADS = 32
SSD_GROUPS = 8
D_STATE = 128
CONV_K = 4
CONV_DIM = D_INNER + 2 * SSD_GROUPS * D_STATE
CHUNK = 128

ATT_DILATIONS = (1, 4, 16)
ATT_GROUPS = 3
ATT_HEADS = 16
ATT_HEAD_DIM = 64
ATT_W = 1024
ATT_BLOCK = 128

MEM_LEN = 256
MEM_HEADS = 4
MEM_HEAD_DIM = 256
MEM_W = 1024

OFF_XBC = D_INNER
OFF_DT = OFF_XBC + CONV_DIM
OFF_QKV = OFF_DT + SSD_HEADS
OFF_ZATT = OFF_QKV + 3 * ATT_GROUPS * ATT_W

CB = 1024
BLK_ZATT = 0
BLK_QMEM = 1
BLK_ZMEM = 2
BLK_GATE = 3
N_SSD_IN = D_INNER + CONV_DIM
N_MAIN = 6 * CB
N_QKV = 3 * ATT_GROUPS * ATT_W
CB_IN = 1536
SSD_STEPS = N_SSD_IN // CB_IN
MAIN_STEPS = N_MAIN // CB_IN
GROUP_STEPS = 3 * ATT_W // CB_IN
MAIN0 = SSD_STEPS
QKV0 = SSD_STEPS + MAIN_STEPS
N_STEPS = QKV0 + ATT_GROUPS * GROUP_STEPS

LANES = 128
N_SLAB = D_MODEL // LANES
DT_PAD = LANES

TM_IN = 1024
TM_MERGE = 512
HALF_MERGE = TM_MERGE // 2
WIN = {1: ATT_BLOCK, 4: 4 * ATT_BLOCK, 16: TM_IN}
Q_STEP = 4 * ATT_BLOCK

VMEM_LIMIT = 62 * 1024 * 1024

NEG_INF = float("-inf")
LOG2E = math.log2(math.e)
Q_SCALE = LOG2E * ATT_HEAD_DIM ** -0.5


def _sigmoid(v):
    return 0.5 * jnp.tanh(0.5 * v) + 0.5


def _silu(v):
    return v * _sigmoid(v)


def _split_dot(lhs_bf16, v):
    hi = v.astype(BF16)
    r1 = v - hi.astype(F32)
    mid = r1.astype(BF16)
    lo = (r1 - mid.astype(F32)).astype(BF16)
    acc = jnp.dot(lhs_bf16, hi, preferred_element_type=F32)
    acc = acc + jnp.dot(lhs_bf16, mid, preferred_element_type=F32)
    return acc + jnp.dot(lhs_bf16, lo, preferred_element_type=F32)


N_PAIR = SSD_HEADS // 2
TAIL = 8
CONV_SLABS = CONV_DIM // LANES
CHUNKS_PER_TILE = TM_IN // CHUNK
MM_PIECE = 256
CONV_SLOTS = 4
CONV_SLABS_PER_SLOT = CONV_SLABS // CONV_SLOTS
N_SLOTS = CONV_SLOTS + SSD_GROUPS


def _ssd_chunk(slab, dt_rows, cw_ref, cb_ref, dtb_ref, alog_ref, dskip_ref, nrm_ref, y_ref,
               state_ref, tail_ref, cbuf_ref, act_ref, interleave):
    for s in range(CONV_SLABS):
        if s % CONV_SLABS_PER_SLOT == 0:
            interleave(s // CONV_SLABS_PER_SLOT)
        sl = slice(s * LANES, (s + 1) * LANES)
        cbuf_ref[s, 0:TAIL, :] = tail_ref[s]
        cbuf_ref[s, TAIL:TAIL + CHUNK, :] = slab(D_INNER + s * LANES).astype(F32)
        conv = cb_ref[:, sl]
        for k in range(CONV_K):
            lo = TAIL - (CONV_K - 1) + k
            conv = conv + cw_ref[k:k + 1, sl] * cbuf_ref[s, pl.ds(lo, CHUNK, stride=1), :]
        tail_ref[s] = cbuf_ref[s, CHUNK:CHUNK + TAIL, :]
        act_ref[:, sl] = _silu(conv)

    v = dt_rows + dtb_ref[...]
    dt = jnp.maximum(v, 0.0) + jnp.log1p(jnp.exp(-jnp.abs(v)))
    a2 = -jnp.exp(alog_ref[...]) * LOG2E
    row = lax.broadcasted_iota(jnp.int32, (CHUNK, CHUNK), 0)
    col = lax.broadcasted_iota(jnp.int32, (CHUNK, CHUNK), 1)
    causal = row >= col
    tri = jnp.where(causal, 1.0, 0.0).astype(BF16)
    acum = _split_dot(tri, dt * a2)
    total = acum[CHUNK - 1:CHUNK, :]
    acum_t = acum.T
    dt_t = dt.T
    dsts_t = (jnp.exp2(total - acum) * dt).T
    cdec = jnp.exp2(total)

    lane = lax.broadcasted_iota(jnp.int32, (CHUNK, LANES), 1)
    first = lane < SSD_HEAD_DIM

    for g in range(SSD_GROUPS):
        interleave(CONV_SLOTS + g)
        b_f = act_ref[:, D_INNER + g * D_STATE:D_INNER + (g + 1) * D_STATE]
        c_f = act_ref[:, D_INNER + (SSD_GROUPS + g) * D_STATE:D_INNER + (SSD_GROUPS + g + 1) * D_STATE]
        b_t = b_f.T
        cbm = lax.dot_general(c_f.astype(BF16), b_f.astype(BF16), (((1,), (1,)), ((), ())),
                              preferred_element_type=F32)
        y_pairs = []
        for pp in range(2):
            pair = 2 * g + pp
            x_pair = act_ref[:, pair * LANES:(pair + 1) * LANES]
            s_prev = state_ref[pair]
            rhs = jnp.concatenate([x_pair, s_prev], axis=0).astype(BF16)
            x_b = x_pair.astype(BF16)
            ys, sts, decs = [], [], []
            for e in range(2):
                h = 2 * pair + e
                acb = jnp.broadcast_to(acum[:, h:h + 1], (CHUNK, CHUNK))
                seg = acb - acum_t[h:h + 1, :]
                decay = jnp.exp2(jnp.where(causal, seg, NEG_INF))
                m = cbm * decay * dt_t[h:h + 1, :]
                c_exp = c_f * jnp.exp2(acb)
                lhs = jnp.concatenate([m, c_exp], axis=1).astype(BF16)
                ys.append(jnp.dot(lhs, rhs, preferred_element_type=F32))
                bts = (b_t * dsts_t[h:h + 1, :]).astype(BF16)
                sts.append(jnp.dot(bts, x_b, preferred_element_type=F32))
                decs.append(jnp.broadcast_to(cdec[:, h:h + 1], (CHUNK, LANES)))
            state_ref[pair] = (s_prev * jnp.where(first, decs[0], decs[1])
                               + jnp.where(first, sts[0], sts[1]))
            y_pair = jnp.where(first, ys[0], ys[1])
            y_pairs.append(y_pair + dskip_ref[:, pair * LANES:(pair + 1) * LANES] * x_pair)

        gsl = slice(2 * g * LANES, (2 * g + 2) * LANES)
        z = jnp.concatenate([slab(2 * g * LANES), slab((2 * g + 1) * LANES)], axis=1).astype(F32)
        yg = jnp.concatenate(y_pairs, axis=1) * _silu(z)
        ms = jnp.mean(yg * yg, axis=-1, keepdims=True)
        y_ref[:, gsl] = (yg * lax.rsqrt(ms + EPS) * nrm_ref[:, gsl]).astype(BF16)


def _make_inproj_ssd_kernel(tiles_per_batch):
    def kernel_body(x_ref, g_ref, w_ref, wdt_ref, cw_ref, cb_ref, dtb_ref, alog_ref, dskip_ref, nrm_ref,
                    main_ref, qkv_ref, y_ref,
                    h_ref, hs_ref, zx_ref, dt_ref, state_ref, tail_ref, cbuf_ref, act_ref):
        i = pl.program_id(0)
        j = pl.program_id(1)

        @pl.when(j == 0)
        def _():
            x = x_ref[...]
            ms = jnp.mean(x * x, axis=-1, keepdims=True)
            h = x * lax.rsqrt(ms + EPS) * g_ref[...]
            hb = h.astype(BF16)
            h_ref[0] = hb
            dt_ref[...] = jnp.dot(hb, wdt_ref[...], preferred_element_type=F32)
            rows16 = TM_IN // 16
            for s in range(N_SLAB):
                sl = slice(s * LANES, (s + 1) * LANES)
                hs_ref[s] = h[:, sl]
                for w in range(TM_IN // WIN[4]):
                    for r in range(4):
                        dst = w * WIN[4] + r * ATT_BLOCK
                        h_ref[1, dst:dst + ATT_BLOCK, sl] = (
                            hs_ref[s, pl.ds(w * WIN[4] + r, ATT_BLOCK, stride=4), :].astype(BF16))
                for r in range(16):
                    h_ref[2, r * rows16:(r + 1) * rows16, sl] = (
                        hs_ref[s, pl.ds(r, rows16, stride=16), :].astype(BF16))

            @pl.when(i % tiles_per_batch == 0)
            def _():
                state_ref[...] = jnp.zeros_like(state_ref)
                tail_ref[...] = jnp.zeros_like(tail_ref)

        def project(which, out_ref, cols, scale=None):
            acc = jnp.dot(h_ref[which], w_ref[:, cols], preferred_element_type=F32)
            if scale is not None:
                acc = acc * scale
            out_ref[:, cols] = acc.astype(BF16)

        q_scale = jnp.where((j - QKV0) % GROUP_STEPS == 0, Q_SCALE, 1.0)

        def with_ssd_chunk(which, out_ref, head_scale=None):
            r0 = pl.multiple_of((j - MAIN0) * CHUNK, CHUNK)

            def slab(c):
                return zx_ref[c // CB_IN, pl.ds(r0, CHUNK), c % CB_IN:c % CB_IN + LANES]

            def interleave(slot):
                k = slot // 2
                if slot % 2 == 0 and k < CB_IN // MM_PIECE:
                    scale = head_scale if (k + 1) * MM_PIECE <= ATT_W else None
                    project(which, out_ref, slice(k * MM_PIECE, (k + 1) * MM_PIECE), scale)

            _ssd_chunk(slab, dt_ref[pl.ds(r0, CHUNK), :], cw_ref, cb_ref, dtb_ref, alog_ref, dskip_ref,
                       nrm_ref, y_ref, state_ref, tail_ref, cbuf_ref, act_ref, interleave)

        group = jnp.maximum(j - QKV0, 0) // GROUP_STEPS

        @pl.when(j < MAIN0)
        def _():
            zx_ref[j] = jnp.dot(h_ref[0], w_ref[...], preferred_element_type=F32).astype(BF16)

        @pl.when(jnp.logical_and(j >= MAIN0, j < QKV0))
        def _():
            with_ssd_chunk(0, main_ref)

        @pl.when(jnp.logical_and(j >= QKV0, j < MAIN0 + CHUNKS_PER_TILE))
        def _():
            with_ssd_chunk(group, qkv_ref, q_scale)

        @pl.when(j >= MAIN0 + CHUNKS_PER_TILE)
        def _():
            project(group, qkv_ref, slice(0, ATT_W), q_scale)
            project(group, qkv_ref, slice(ATT_W, CB_IN))

    return kernel_body


def _inproj_ssd(x2, gain, w_all, w_dt, conv_w, conv_b, dt_bias, a_log, d_skip_x, ssd_norm, seq):
    t = x2.shape[0]
    const = lambda i, j: (0, 0)
    return pl.pallas_call(
        _make_inproj_ssd_kernel(seq // TM_IN),
        grid=(t // TM_IN, N_STEPS),
        in_specs=[
            pl.BlockSpec((TM_IN, D_MODEL), lambda i, j: (i, 0)),
            pl.BlockSpec((1, D_MODEL), const),
            pl.BlockSpec((D_MODEL, CB_IN), lambda i, j: (0, j)),
            pl.BlockSpec((D_MODEL, DT_PAD), const),
            pl.BlockSpec((CONV_K, CONV_DIM), const),
            pl.BlockSpec((1, CONV_DIM), const),
            pl.BlockSpec((1, DT_PAD), const),
            pl.BlockSpec((1, DT_PAD), const),
            pl.BlockSpec((1, D_INNER), const),
            pl.BlockSpec((1, D_INNER), const),
        ],
        out_specs=[
            pl.BlockSpec((TM_IN, CB_IN), lambda i, j: (i, jnp.clip(j - MAIN0, 0, MAIN_STEPS - 1))),
            pl.BlockSpec((TM_IN, CB_IN), lambda i, j: (i, jnp.maximum(j - QKV0, 0))),
            pl.BlockSpec((CHUNK, D_INNER),
                         lambda i, j: (i * CHUNKS_PER_TILE + jnp.clip(j - MAIN0, 0, CHUNKS_PER_TILE - 1), 0)),
        ],
        out_shape=[
            jax.ShapeDtypeStruct((t, N_MAIN), BF16),
            jax.ShapeDtypeStruct((t, N_QKV), BF16),
            jax.ShapeDtypeStruct((t, D_INNER), BF16),
        ],
        scratch_shapes=[
            pltpu.VMEM((3, TM_IN, D_MODEL), BF16),
            pltpu.VMEM((N_SLAB, TM_IN, LANES), F32),
            pltpu.VMEM((SSD_STEPS, TM_IN, CB_IN), BF16),
            pltpu.VMEM((TM_IN, DT_PAD), F32),
            pltpu.VMEM((N_PAIR, D_STATE, LANES), F32),
            pltpu.VMEM((CONV_SLABS, TAIL, LANES), F32),
            pltpu.VMEM((CONV_SLABS, TAIL + CHUNK, LANES), F32),
            pltpu.VMEM((CHUNK, CONV_DIM), F32),
        ],
        compiler_params=pltpu.CompilerParams(
            dimension_semantics=("arbitrary", "arbitrary"), vmem_limit_bytes=VMEM_LIMIT),
        name="inproj_ssd",
    )(x2, gain, w_all, w_dt, conv_w, conv_b, dt_bias, a_log, d_skip_x, ssd_norm)


def _memkv_kernel(m_ref, g_ref, w_ref, out_ref):
    m = m_ref[...]
    ms = jnp.mean(m * m, axis=-1, keepdims=True)
    h = (m * lax.rsqrt(ms + EPS) * g_ref[...]).astype(BF16)
    out_ref[...] = jnp.dot(h, w_ref[...], preferred_element_type=F32).astype(BF16)


def _memkv(mem2, gain, w_kv):
    rows = mem2.shape[0]
    return pl.pallas_call(
        _memkv_kernel,
        grid=(2 * MEM_W // CB,),
        in_specs=[
            pl.BlockSpec((rows, D_MODEL), lambda j: (0, 0)),
            pl.BlockSpec((1, D_MODEL), lambda j: (0, 0)),
            pl.BlockSpec((D_MODEL, CB), lambda j: (0, j)),
        ],
        out_specs=pl.BlockSpec((rows, CB), lambda j: (0, j)),
        out_shape=jax.ShapeDtypeStruct((rows, 2 * MEM_W), BF16),
        compiler_params=pltpu.CompilerParams(
            dimension_semantics=("arbitrary",), vmem_limit_bytes=VMEM_LIMIT),
        name="memkv",
    )(mem2, gain, w_kv)


def _make_attn_kernel(rows):
    parts = ATT_BLOCK // rows
    blocks = Q_STEP // ATT_BLOCK

    def load(ref, blk, sl):
        pieces = [ref[blk * parts + p, :, sl] for p in range(parts)]
        return pieces[0] if parts == 1 else jnp.concatenate(pieces, axis=0)

    def store(ref, blk, sl, val):
        for p in range(parts):
            ref[blk * parts + p, :, sl] = val[p * rows:(p + 1) * rows]

    def attn_kernel(q_ref, k_ref, v_ref, o_ref, lse_ref, kp_ref, vp_ref):
        n = pl.program_id(2)

        @pl.when(n == 0)
        def _():
            kp_ref[...] = jnp.zeros_like(kp_ref)
            vp_ref[...] = jnp.zeros_like(vp_ref)

        qi = lax.broadcasted_iota(jnp.int32, (ATT_BLOCK, 2 * ATT_BLOCK), 0)
        kj = lax.broadcasted_iota(jnp.int32, (ATT_BLOCK, 2 * ATT_BLOCK), 1)
        band = jnp.logical_and(kj >= qi, kj - ATT_BLOCK <= qi)
        lane = lax.broadcasted_iota(jnp.int32, (ATT_BLOCK, LANES), 1)
        first = lane < ATT_HEAD_DIM
        first_kv = lax.broadcasted_iota(jnp.int32, (2 * ATT_BLOCK, LANES), 1) < ATT_HEAD_DIM

        def block_body(blk, carry):
            first_key = jnp.where(n * blocks + blk > 0, 0, ATT_BLOCK)
            valid = jnp.logical_and(band, kj >= first_key)
            pairs, scores, probs, results, maxes = {}, {}, {}, {}, {}

            def load_pair(hp):
                sl = slice(hp * LANES, (hp + 1) * LANES)
                k_cur = load(k_ref, blk, sl)
                v_cur = load(v_ref, blk, sl)
                kk = jnp.concatenate([kp_ref[:, sl], k_cur], axis=0)
                vv = jnp.concatenate([vp_ref[:, sl], v_cur], axis=0)
                kp_ref[:, sl] = k_cur
                vp_ref[:, sl] = v_cur
                ones = jnp.ones_like(vv)
                v_ones = (jnp.where(first_kv, vv, ones), jnp.where(first_kv, ones, vv))
                pairs[hp] = (load(q_ref, blk, sl), kk, v_ones)

            def score(h):
                hp, e = divmod(h, 2)
                if e == 0:
                    load_pair(hp)
                q2, kk, _ = pairs[hp]
                qh = jnp.where(first if e == 0 else jnp.logical_not(first), q2, jnp.zeros_like(q2))
                scores[h] = lax.dot_general(qh, kk, (((1,), (1,)), ((), ())), preferred_element_type=F32)

            def softmax_numerator(h):
                sc = jnp.where(valid, scores.pop(h), NEG_INF)
                maxes[h] = jnp.max(sc, axis=-1, keepdims=True)
                probs[h] = jnp.exp2(sc - maxes[h]).astype(BF16)

            def weighted_values(h):
                hp, e = divmod(h, 2)
                results[h] = jnp.dot(probs.pop(h), pairs[hp][2][e], preferred_element_type=F32)

            def finish_pair(hp, lse_tile):
                r0, r1 = results.pop(2 * hp), results.pop(2 * hp + 1)
                den_other = jnp.where(first, r1, r0)
                den = pltpu.roll(den_other, ATT_HEAD_DIM, 1)
                o = jnp.where(first, r0, r1) / den
                store(o_ref, blk, slice(hp * LANES, (hp + 1) * LANES), o.astype(BF16))
                lse_tile = jnp.where(lane == 2 * hp, maxes.pop(2 * hp) + jnp.log2(den), lse_tile)
                lse_tile = jnp.where(lane == 2 * hp + 1, maxes.pop(2 * hp + 1) + jnp.log2(den_other), lse_tile)
                del pairs[hp]
                return lse_tile

            lse_tile = jnp.zeros((ATT_BLOCK, LANES), F32)
            for t in range(ATT_HEADS + 3):
                if t < ATT_HEADS:
                    score(t)
                if 0 <= t - 1 < ATT_HEADS:
                    softmax_numerator(t - 1)
                if 0 <= t - 2 < ATT_HEADS:
                    weighted_values(t - 2)
                if 0 <= t - 3 < ATT_HEADS and (t - 3) % 2 == 1:
                    lse_tile = finish_pair((t - 3) // 2, lse_tile)
            store(lse_ref, blk, slice(0, LANES), lse_tile)
            return carry

        lax.fori_loop(0, blocks, block_body, 0)

    return attn_kernel


def _attention(qkv2, bsz, seq, g):
    d = ATT_DILATIONS[g]
    rows = WIN[d] // d
    lead = (bsz, seq // WIN[d], d, rows)
    pieces = Q_STEP // rows
    view = qkv2.reshape(lead + (N_QKV,))
    blk = (None, pieces, None, rows, ATT_W)

    def in_spec(kind):
        return pl.BlockSpec(blk, lambda i, r, n: (i, n, r, 0, 3 * g + kind))

    out_map = lambda i, r, n: (i, n, r, 0, 0)
    return pl.pallas_call(
        _make_attn_kernel(rows),
        grid=(bsz, d, seq // d // Q_STEP),
        in_specs=[in_spec(kind) for kind in range(3)],
        out_specs=[pl.BlockSpec(blk, out_map), pl.BlockSpec(blk[:-1] + (LANES,), out_map)],
        out_shape=[
            jax.ShapeDtypeStruct(lead + (ATT_W,), BF16),
            jax.ShapeDtypeStruct(lead + (LANES,), F32),
        ],
        scratch_shapes=[
            pltpu.VMEM((ATT_BLOCK, ATT_W), BF16),
            pltpu.VMEM((ATT_BLOCK, ATT_W), BF16),
        ],
        compiler_params=pltpu.CompilerParams(
            dimension_semantics=("parallel", "parallel", "arbitrary"), vmem_limit_bytes=VMEM_LIMIT),
        name=f"attn_d{d}",
    )(view, view, view)


def _merge_kernel(x_ref, yssd_ref, o1_ref, o4_ref, o16_ref, l1_ref, l4_ref, l16_ref,
                  zatt_ref, qmem_ref, zmem_ref, g0_ref, g1_ref, g2_ref, kv_ref,
                  wso_ref, wao_ref, wmo_ref, wout_ref, npost_ref, p4_ref, p16_ref, ex_ref,
                  out_ref, tok_ref):
    n4, n16 = HALF_MERGE // 4, HALF_MERGE // 16

    def per_channel(wt):
        hi = wt.astype(BF16)
        lo = (wt - hi.astype(F32)).astype(BF16)
        return jnp.dot(jnp.concatenate([hi, lo], axis=1), ex_ref[...], preferred_element_type=F32)

    def steps(hf):
        rs = slice(hf * HALF_MERGE, (hf + 1) * HALF_MERGE)
        v = {}

        def weights():
            for r in range(4):
                tok_ref[2 * hf, pl.ds(r, n4, stride=4), :] = l4_ref[r, hf * n4:(hf + 1) * n4, :]
            for r in range(16):
                tok_ref[2 * hf + 1, pl.ds(r, n16, stride=16), :] = l16_ref[r, hf * n16:(hf + 1) * n16, :]
            la, lb, lc = l1_ref[rs, :], tok_ref[2 * hf], tok_ref[2 * hf + 1]
            top = jnp.maximum(la, jnp.maximum(lb, lc))
            ea, eb, ec = jnp.exp2(la - top), jnp.exp2(lb - top), jnp.exp2(lc - top)
            inv = 1.0 / (ea + eb + ec)
            v["w4"], v["w16"] = per_channel(eb * inv), per_channel(ec * inv)
            o4 = o4_ref[:, hf * n4:(hf + 1) * n4, :].reshape(HALF_MERGE, ATT_W)
            o16 = o16_ref[:, hf * n16:(hf + 1) * n16, :].reshape(HALF_MERGE, ATT_W)
            v["o4"] = jnp.dot(p4_ref[...], o4, preferred_element_type=F32)
            v["o16"] = jnp.dot(p16_ref[...], o16, preferred_element_type=F32)

        def combine():
            w4, w16 = v.pop("w4"), v.pop("w16")
            o = ((1.0 - w4 - w16) * o1_ref[rs, :].astype(F32) + w4 * v.pop("o4")
                 + w16 * v.pop("o16"))
            v["y_att"] = (o * _silu(zatt_ref[rs, :].astype(F32))).astype(BF16)

        def ssd_att_proj():
            v["ssd"] = jnp.dot(yssd_ref[rs, :], wso_ref[...], preferred_element_type=F32)
            v["att"] = jnp.dot(v.pop("y_att"), wao_ref[...], preferred_element_type=F32)

        def mem_attention():
            heads = []
            for h in range(MEM_HEADS):
                sl = slice(h * MEM_HEAD_DIM, (h + 1) * MEM_HEAD_DIM)
                q = qmem_ref[rs, sl] * (MEM_HEAD_DIM ** -0.5)
                k = kv_ref[0, :, sl]
                vv = kv_ref[0, :, MEM_W + h * MEM_HEAD_DIM:MEM_W + (h + 1) * MEM_HEAD_DIM]
                sc = lax.dot_general(q, k, (((1,), (1,)), ((), ())), preferred_element_type=F32)
                mx = jnp.max(sc, axis=-1, keepdims=True)
                p = jnp.exp(sc - mx)
                den = jnp.sum(p, axis=-1, keepdims=True)
                heads.append(jnp.dot(p.astype(BF16), vv, preferred_element_type=F32) / den)
            o_mem = jnp.concatenate(heads, axis=1)
            v["y_mem"] = (o_mem * _silu(zmem_ref[rs, :].astype(F32))).astype(BF16)

        def mem_proj():
            v["mem"] = jnp.dot(v.pop("y_mem"), wmo_ref[...], preferred_element_type=F32)

        def gates():
            merged = (_sigmoid(g0_ref[rs, :].astype(F32)) * v.pop("ssd")
                      + _sigmoid(g1_ref[rs, :].astype(F32)) * v.pop("att")
                      + _sigmoid(g2_ref[rs, :].astype(F32)) * v.pop("mem"))
            v["merged"] = merged.astype(BF16)

        def out_proj():
            v["out"] = jnp.dot(v.pop("merged"), wout_ref[...], preferred_element_type=F32)

        def finish():
            out = v.pop("out")
            ms = jnp.mean(out * out, axis=-1, keepdims=True)
            out_ref[rs, :] = x_ref[rs, :] + out * lax.rsqrt(ms + EPS) * npost_ref[...]

        return [weights, combine, ssd_att_proj, mem_attention, mem_proj, gates, out_proj, finish]

    first_half, second_half = steps(0), steps(1)
    for t in range(len(first_half) + 1):
        if t < len(first_half):
            first_half[t]()
        if t >= 1:
            second_half[t - 1]()


def _stream_to_token_matrix(d, tm):
    tok = jnp.arange(tm)
    src = (tok % d) * (tm // d) + tok // d
    return (src[:, None] == jnp.arange(tm)[None, :]).astype(BF16)


def _head_to_channel_matrix():
    lane = jnp.arange(2 * LANES) % LANES
    chan_head = jnp.arange(ATT_W) // ATT_HEAD_DIM
    return (lane[:, None] == chan_head[None, :]).astype(BF16)


def _merge(x2, y_ssd, o1, l1, o4, l4, o16, l16, main2, kv3, w_so, w_ao, w_mo, w_out, n_post, seq):
    t = x2.shape[0]
    tm = TM_MERGE
    per_batch = seq // tm
    row = lambda i: (i, 0)
    const = lambda i: (0, 0)

    def mblk(j):
        return pl.BlockSpec((tm, CB), lambda i: (i, j))

    def stream_spec(d, width):
        per_win = WIN[d] // tm
        return pl.BlockSpec((None, None, d, tm // d, width),
                            lambda i: (i // per_batch, (i % per_batch) // per_win, 0, i % per_win, 0))

    return pl.pallas_call(
        _merge_kernel,
        grid=(t // tm,),
        in_specs=[
            pl.BlockSpec((tm, D_MODEL), row),
            pl.BlockSpec((tm, D_INNER), row),
            pl.BlockSpec((tm, ATT_W), row), stream_spec(4, ATT_W), stream_spec(16, ATT_W),
            pl.BlockSpec((tm, LANES), row), stream_spec(4, LANES), stream_spec(16, LANES),
            mblk(BLK_ZATT), mblk(BLK_QMEM), mblk(BLK_ZMEM),
            mblk(BLK_GATE), mblk(BLK_GATE + 1), mblk(BLK_GATE + 2),
            pl.BlockSpec((1, MEM_LEN, 2 * MEM_W), lambda i: (i // per_batch, 0, 0)),
            pl.BlockSpec((D_INNER, D_MODEL), const),
            pl.BlockSpec((ATT_W, D_MODEL), const),
            pl.BlockSpec((MEM_W, D_MODEL), const),
            pl.BlockSpec((D_MODEL, D_MODEL), const),
            pl.BlockSpec((1, D_MODEL), const),
            pl.BlockSpec((HALF_MERGE, HALF_MERGE), const),
            pl.BlockSpec((HALF_MERGE, HALF_MERGE), const),
            pl.BlockSpec((2 * LANES, ATT_W), const),
        ],
        out_specs=pl.BlockSpec((tm, D_MODEL), row),
        out_shape=jax.ShapeDtypeStruct((t, D_MODEL), F32),
        scratch_shapes=[pltpu.VMEM((4, HALF_MERGE, LANES), F32)],
        compiler_params=pltpu.CompilerParams(
            dimension_semantics=("parallel",), vmem_limit_bytes=VMEM_LIMIT),
        name="merge",
    )(x2, y_ssd, o1, o4, o16, l1, l4, l16, main2, main2, main2, main2, main2, main2, kv3,
      w_so, w_ao, w_mo, w_out, n_post, _stream_to_token_matrix(4, HALF_MERGE),
      _stream_to_token_matrix(16, HALF_MERGE),
      _head_to_channel_matrix())


def _pad_lanes(v, width):
    return jnp.pad(v, ((0, 0), (0, width - v.shape[-1])))


def _regroup_w_in(w):
    cols = [w[:, :OFF_DT], w[:, OFF_ZATT:]]
    for g in range(ATT_GROUPS):
        for kind in range(3):
            start = OFF_QKV + (kind * ATT_GROUPS + g) * ATT_W
            cols.append(w[:, start:start + ATT_W])
    return jnp.concatenate(cols, axis=1).astype(BF16)


def kernel(x, mem, norm_pre, norm_post, w_in, conv_w, conv_b, dt_bias, a_log, d_skip, ssd_norm,
           w_ssd_out, w_attn_out, mem_norm, w_mem_kv, w_mem_out, w_out):
    bsz, seq, _ = x.shape
    depth = w_in.shape[0]
    t = bsz * seq
    assert seq % (16 * Q_STEP) == 0 and seq % TM_IN == 0 and WIN[4] % TM_MERGE == 0

    x2 = x.reshape(t, D_MODEL)
    mem2 = mem.reshape(bsz * MEM_LEN, D_MODEL)
    for l in range(depth):
        w = w_in[l]
        w_dt = _pad_lanes(w[:, OFF_DT:OFF_QKV], DT_PAD).astype(BF16)
        main2, qkv2, y_ssd = _inproj_ssd(
            x2, norm_pre[l][None, :], _regroup_w_in(w), w_dt, conv_w[l], conv_b[l][None, :],
            _pad_lanes(dt_bias[l][None, :], DT_PAD), _pad_lanes(a_log[l][None, :], DT_PAD),
            jnp.repeat(d_skip[l], SSD_HEAD_DIM)[None, :], ssd_norm[l][None, :], seq)
        kv = _memkv(mem2, mem_norm[l][None, :], w_mem_kv[l].astype(BF16))

        o1, l1 = _attention(qkv2, bsz, seq, 0)
        o4, l4 = _attention(qkv2, bsz, seq, 1)
        o16, l16 = _attention(qkv2, bsz, seq, 2)

        x2 = _merge(x2, y_ssd, o1.reshape(t, ATT_W), l1.reshape(t, LANES),
                    o4, l4, o16, l16, main2, kv.reshape(bsz, MEM_LEN, 2 * MEM_W),
                    w_ssd_out[l].astype(BF16), w_attn_out[l].astype(BF16),
                    w_mem_out[l].astype(BF16), w_out[l].astype(BF16), norm_post[l][None, :], seq)
    return x2.reshape(bsz, seq, D_MODEL)
```
